```python
import math
import jax, jax.numpy as jnp
from jax import lax
import numpy as np

D_MODEL = 4096
BATCH = 4
SEQ = 4096
DEPTH = 1
DEC_BATCH = 16
DEC_SEQ = 64
PAST_LEN = 4096

CHUNK = 64
MIX_WIDTH = D_MODEL
CONV_CH = MIX_WIDTH // 2
CONV_WIDTH = 31
CONV_STATE = CONV_WIDTH - 1
HEAD_DIM = 128
N_HEADS = (MIX_WIDTH - CONV_CH) // HEAD_DIM
N_KV_HEADS = 4
GQA_GROUP = N_HEADS // N_KV_HEADS
ROPE_DIM = HEAD_DIM // 4
ROPE_THETA = 500000.0
IDX_HEADS = 32
IDX_DIM = 128
IDX_ROPE_DIM = IDX_DIM // 4
IDX_SCALE = (IDX_HEADS ** -0.5) * (IDX_DIM ** -0.5)
TOPK_MAX = 256
ATTN_BLOCK = 128
ATTN_SCALE = HEAD_DIM ** -0.5
MEM_TOKENS = 256
MEM_HEADS = 4
MEM_DIM = MEM_HEADS * HEAD_DIM
PEER_KEYS = 128
PEER_EXPERTS = PEER_KEYS * PEER_KEYS
PEER_HEADS = 8
PEER_QDIM = 256
PEER_HALF = PEER_QDIM // 2
PEER_TOPK = 16
PEER_BLOCK = 128
EPS = 1e-6

OFF_Q = 2 * CONV_CH
OFF_K = OFF_Q + N_HEADS * HEAD_DIM
OFF_V = OFF_K + N_KV_HEADS * HEAD_DIM
OFF_QI = OFF_V + N_KV_HEADS * HEAD_DIM
OFF_KI = OFF_QI + IDX_HEADS * IDX_DIM
OFF_WI = OFF_KI + IDX_DIM
IN_COLS = OFF_WI + IDX_HEADS
IN_SPLITS = (OFF_Q, OFF_K, OFF_V, OFF_QI, OFF_KI, OFF_WI)

kernel_name = "hybrid_conv_dsa_peer_stream_step"


def _rmsnorm(x, g):
    xf = x.astype(jnp.float32)
    y = xf * lax.rsqrt(jnp.mean(xf * xf, axis=-1, keepdims=True) + EPS)
    return (y * g.astype(jnp.float32)).astype(x.dtype)


def _layernorm(x, g, b):
    xf = x.astype(jnp.float32)
    mu = jnp.mean(xf, axis=-1, keepdims=True)
    var = jnp.mean(jnp.square(xf - mu), axis=-1, keepdims=True)
    y = (xf - mu) * lax.rsqrt(var + EPS)
    return (y * g.astype(jnp.float32) + b.astype(jnp.float32)).astype(x.dtype)


def _rope(x, pos, rot_dim):
    half = rot_dim // 2
    inv_freq = jnp.power(ROPE_THETA, -jnp.arange(half, dtype=jnp.float32) / half)
    ang = pos.astype(jnp.float32)[:, None] * inv_freq[None, :]
    cos = jnp.cos(ang)[:, None, :].astype(x.dtype)
    sin = jnp.sin(ang)[:, None, :].astype(x.dtype)
    x1 = x[..., :half]
    x2 = x[..., half:rot_dim]
    return jnp.concatenate([x1 * cos - x2 * sin, x2 * cos + x1 * sin, x[..., rot_dim:]], axis=-1)


def _conv_module(glu, conv_prev, dw_w, dw_b, ln_g, ln_b):
    a, g = jnp.split(glu, 2, axis=-1)
    u = a * jax.nn.sigmoid(g)
    if conv_prev is None:
        conv_prev = jnp.zeros((u.shape[0], CONV_STATE, CONV_CH), u.dtype)
    padded = jnp.concatenate([conv_prev.astype(u.dtype), u], axis=1)
    c = lax.conv_general_dilated(padded, dw_w[:, None, :].astype(u.dtype), (1,), 'VALID',
                                 dimension_numbers=('NWC', 'WIO', 'NWC'),
                                 feature_group_count=CONV_CH) + dw_b
    c = _layernorm(c, ln_g, ln_b)
    return jax.nn.silu(c), padded[:, -CONV_STATE:]


def _gather_rows(a, idx):
    return jax.vmap(lambda ab, ib: ab[ib])(a, idx)


def _dsa_attention(q, k, v, qi, ki, wi, q_pos, k_pos, topk):
    B, T = q.shape[0], q.shape[1]
    qb = min(ATTN_BLOCK, T)
    nb = T // qb
    k_chunk = k_pos // CHUNK

    def blockify(a):
        return jnp.moveaxis(a.reshape((B, nb, qb) + a.shape[2:]), 1, 0)

    def one_block(args):
        q_b, qi_b, wi_b, pos_b = args
        q_chunk = pos_b // CHUNK
        s = jnp.einsum('bthd,bsd->bths', qi_b, ki)
        score = jnp.einsum('bth,bths->bts', wi_b, jax.nn.relu(s)).astype(jnp.float32) * IDX_SCALE
        adm = k_chunk[None, :] <= q_chunk[:, None]
        score = jnp.where(adm[None], score, -jnp.inf)
        _, idx = lax.top_k(score, topk)
        valid = k_chunk[idx] <= q_chunk[None, :, None]
        ks = _gather_rows(k, idx)
        vs = _gather_rows(v, idx)
        qg = q_b.reshape(B, qb, N_KV_HEADS, GQA_GROUP, HEAD_DIM)
        logits = jnp.einsum('btgrd,btkgd->btgrk', qg, ks).astype(jnp.float32) * ATTN_SCALE
        logits = jnp.where(valid[:, :, None, None, :], logits, -jnp.inf)
        p = jax.nn.softmax(logits, axis=-1).astype(vs.dtype)
        o = jnp.einsum('btgrk,btkgd->btgrd', p, vs)
        return o.reshape(B, qb, N_HEADS * HEAD_DIM)

    out = lax.map(one_block, (blockify(q), blockify(qi), blockify(wi), q_pos.reshape(nb, qb)))
    return jnp.moveaxis(out, 0, 1).reshape(B, T, N_HEADS * HEAD_DIM)


def _mem_kv(mem, mem_norm_g, w_k_mem, w_v_mem, mem_k_norm_g):
    B, M, _ = mem.shape
    mn = _rmsnorm(mem, mem_norm_g)
    mk = _rmsnorm((mn @ w_k_mem).reshape(B, M, MEM_HEADS, HEAD_DIM), mem_k_norm_g)
    mv = (mn @ w_v_mem).reshape(B, M, MEM_HEADS, HEAD_DIM)
    return mk, mv


def _mem_attention(hn, mem_k, mem_v, w_q_mem, mem_q_norm_g, w_o_mem):
    B, T, _ = hn.shape
    q = _rmsnorm((hn @ w_q_mem).reshape(B, T, MEM_HEADS, HEAD_DIM), mem_q_norm_g)
    logits = jnp.einsum('bthd,bmhd->bhtm', q, mem_k).astype(jnp.float32) * ATTN_SCALE
    p = jax.nn.softmax(logits, axis=-1).astype(mem_v.dtype)
    o = jnp.einsum('bhtm,bmhd->bthd', p, mem_v).reshape(B, T, MEM_DIM)
    return o @ w_o_mem


def _peer(hn, peer_wq, sub_k1, sub_k2, u_tab, v_tab):
    B, T, D = hn.shape
    n = B * T
    nb = -(-n // PEER_BLOCK)
    xf = jnp.pad(hn.reshape(n, D), ((0, nb * PEER_BLOCK - n), (0, 0)))

    def one_block(xb):
        q = (xb @ peer_wq).reshape(PEER_BLOCK, PEER_HEADS, PEER_QDIM)
        s1 = jnp.einsum('thd,hkd->thk', q[..., :PEER_HALF], sub_k1).astype(jnp.float32)
        s2 = jnp.einsum('thd,hkd->thk', q[..., PEER_HALF:], sub_k2).astype(jnp.float32)
        v1, i1 = lax.top_k(s1, PEER_TOPK)
        v2, i2 = lax.top_k(s2, PEER_TOPK)
        cand = (v1[..., :, None] + v2[..., None, :]).reshape(PEER_BLOCK, PEER_HEADS, PEER_TOPK * PEER_TOPK)
        cand_id = (i1[..., :, None] * PEER_KEYS + i2[..., None, :]).reshape(PEER_BLOCK, PEER_HEADS, PEER_TOPK * PEER_TOPK)
        top_s, top_pos = lax.top_k(cand, PEER_TOPK)
        eid = jnp.take_along_axis(cand_id, top_pos, axis=-1)
        g = jax.nn.softmax(top_s, axis=-1)
        u_sel = u_tab[eid]
        act = jax.nn.gelu(jnp.einsum('thkd,td->thk', u_sel, xb), approximate=False)
        coef = (g * act.astype(jnp.float32)).astype(xb.dtype)
        return jnp.einsum('thk,thkd->td', coef, v_tab[eid])

    out = lax.map(one_block, xf.reshape(nb, PEER_BLOCK, D)).reshape(nb * PEER_BLOCK, D)
    return out[:n].reshape(B, T, D)


def _layer(x, pos, k_pos, topk, conv_prev, k_past, v_past, ki_past, mem_k, mem_v,
           norm_mix_g, w_in, dw_w, dw_b, conv_ln_g, conv_ln_b, q_norm_g, k_norm_g, w_out,
           norm_mem_g, w_q_mem, mem_q_norm_g, w_o_mem,
           norm_ffn_g, peer_wq, peer_sub_k1, peer_sub_k2, peer_u, peer_v):
    B, T, _ = x.shape
    hn = _rmsnorm(x, norm_mix_g)
    glu, q, k, v, qi, ki, wi = jnp.split(hn @ w_in, IN_SPLITS, axis=-1)
    conv_out, conv_new = _conv_module(glu, conv_prev, dw_w, dw_b, conv_ln_g, conv_ln_b)
    q = _rope(_rmsnorm(q.reshape(B, T, N_HEADS, HEAD_DIM), q_norm_g), pos, ROPE_DIM)
    k = _rope(_rmsnorm(k.reshape(B, T, N_KV_HEADS, HEAD_DIM), k_norm_g), pos, ROPE_DIM)
    v = v.reshape(B, T, N_KV_HEADS, HEAD_DIM)
    qi = _rope(qi.reshape(B, T, IDX_HEADS, IDX_DIM), pos, IDX_ROPE_DIM)
    ki = _rope(ki[:, :, None, :], pos, IDX_ROPE_DIM)[:, :, 0]
    if k_past is None:
        k_all, v_all, ki_all = k, v, ki
    else:
        k_all = jnp.concatenate([k_past.astype(k.dtype), k], axis=1)
        v_all = jnp.concatenate([v_past.astype(v.dtype), v], axis=1)
        ki_all = jnp.concatenate([ki_past.astype(ki.dtype), ki], axis=1)
    attn_out = _dsa_attention(q, k_all, v_all, qi, ki_all, wi, pos, k_pos, topk)
    h = x + jnp.concatenate([conv_out, attn_out], axis=-1) @ w_out
    h = h + _mem_attention(_rmsnorm(h, norm_mem_g), mem_k, mem_v, w_q_mem, mem_q_norm_g, w_o_mem)
    y = h + _peer(_rmsnorm(h, norm_ffn_g), peer_wq, peer_sub_k1, peer_sub_k2, peer_u, peer_v)
    return y, k, v, ki, conv_new


def setup_inputs(seed: int = 0) -> dict:
    key = jax.random.key(seed)
    ks = iter(jax.random.split(key, 40))

    def nrm(shape, scale):
        return jax.random.normal(next(ks), shape, jnp.float32) * scale

    def gain(shape):
        return 1.0 + nrm(shape, 0.02)

    L = DEPTH
    return {
        "x_prompt": nrm((BATCH, SEQ, D_MODEL), 1.0),
        "x_sample": nrm((DEC_BATCH, DEC_SEQ, D_MODEL), 1.0),
        "mem_prompt": nrm((BATCH, MEM_TOKENS, D_MODEL), 1.0),
        "cache_k": nrm((L, DEC_BATCH, PAST_LEN, N_KV_HEADS, HEAD_DIM), 1.0),
        "cache_v": nrm((L, DEC_BATCH, PAST_LEN, N_KV_HEADS, HEAD_DIM), 1.0),
        "cache_k_idx": nrm((L, DEC_BATCH, PAST_LEN, IDX_DIM), 1.0),
        "state_conv": nrm((L, DEC_BATCH, CONV_STATE, CONV_CH), 0.5),
        "cache_mem_k": nrm((L, DEC_BATCH, MEM_TOKENS, MEM_HEADS, HEAD_DIM), 1.0),
        "cache_mem_v": nrm((L, DEC_BATCH, MEM_TOKENS, MEM_HEADS, HEAD_DIM), 1.0),
        "norm_mix_g": gain((L, D_MODEL)),
        "w_in": nrm((L, D_MODEL, IN_COLS), D_MODEL ** -0.5),
        "dw_w": nrm((L, CONV_WIDTH, CONV_CH), CONV_WIDTH ** -0.5),
        "dw_b": nrm((L, CONV_CH), 0.02),
        "conv_ln_g": gain((L, CONV_CH)),
        "conv_ln_b": nrm((L, CONV_CH), 0.02),
        "q_norm_g": gain((L, HEAD_DIM)),
        "k_norm_g": gain((L, HEAD_DIM)),
        "w_out": nrm((L, MIX_WIDTH, D_MODEL), MIX_WIDTH ** -0.5),
        "norm_mem_g": gain((L, D_MODEL)),
        "mem_norm_g": gain((L, D_MODEL)),
        "w_q_mem": nrm((L, D_MODEL, MEM_DIM), D_MODEL ** -0.5),
        "w_k_mem": nrm((L, D_MODEL, MEM_DIM), D_MODEL ** -0.5),
        "w_v_mem": nrm((L, D_MODEL, MEM_DIM), D_MODEL ** -0.5),
        "mem_q_norm_g": gain((L, HEAD_DIM)),
        "mem_k_norm_g": gain((L, HEAD_DIM)),
        "w_o_mem": nrm((L, MEM_DIM, D_MODEL), MEM_DIM ** -0.5),
        "norm_ffn_g": gain((L, D_MODEL)),
        "peer_wq": nrm((L, D_MODEL, PEER_HEADS * PEER_QDIM), D_MODEL ** -0.5),
        "peer_sub_k1": nrm((L, PEER_HEADS, PEER_KEYS, PEER_HALF), PEER_HALF ** -0.5),
        "peer_sub_k2": nrm((L, PEER_HEADS, PEER_KEYS, PEER_HALF), PEER_HALF ** -0.5),
        "peer_u": nrm((L, PEER_EXPERTS, D_MODEL), D_MODEL ** -0.5),
        "peer_v": nrm((L, PEER_EXPERTS, D_MODEL), 0.2),
    }


def reference(x_prompt, x_sample, mem_prompt, cache_k, cache_v, cache_k_idx, state_conv,
              cache_mem_k, cache_mem_v,
              norm_mix_g, w_in, dw_w, dw_b, conv_ln_g, conv_ln_b, q_norm_g, k_norm_g, w_out,
              norm_mem_g, mem_norm_g, w_q_mem, w_k_mem, w_v_mem, mem_q_norm_g, mem_k_norm_g, w_o_mem,
              norm_ffn_g, peer_wq, peer_sub_k1, peer_sub_k2, peer_u, peer_v):
    T_p = x_prompt.shape[1]
    T_s = x_sample.shape[1]
    pos_p = jnp.arange(T_p, dtype=jnp.int32)
    pos_s = PAST_LEN + jnp.arange(T_s, dtype=jnp.int32)
    kpos_s = jnp.arange(PAST_LEN + T_s, dtype=jnp.int32)
    topk_p = min(TOPK_MAX, T_p // 4)
    topk_s = min(TOPK_MAX, (PAST_LEN + T_s) // 4)

    h_p, h_s = x_prompt, x_sample
    kp_l, vp_l, kip_l, cp_l, mkp_l, mvp_l = [], [], [], [], [], []
    ks_l, vs_l, kis_l, cs_l = [], [], [], []
    for l in range(DEPTH):
        lp = (norm_mix_g[l], w_in[l], dw_w[l], dw_b[l], conv_ln_g[l], conv_ln_b[l], q_norm_g[l], k_norm_g[l],
              w_out[l], norm_mem_g[l], w_q_mem[l], mem_q_norm_g[l], w_o_mem[l],
              norm_ffn_g[l], peer_wq[l], peer_sub_k1[l], peer_sub_k2[l], peer_u[l], peer_v[l])
        mk_p, mv_p = _mem_kv(mem_prompt, mem_norm_g[l], w_k_mem[l], w_v_mem[l], mem_k_norm_g[l])
        h_p, kp, vp, kip, cp = _layer(h_p, pos_p, pos_p, topk_p, None, None, None, None, mk_p, mv_p, *lp)
        h_s, ks_, vs_, kis, cs = _layer(h_s, pos_s, kpos_s, topk_s, state_conv[l], cache_k[l], cache_v[l],
                                        cache_k_idx[l], cache_mem_k[l].astype(h_s.dtype),
                                        cache_mem_v[l].astype(h_s.dtype), *lp)
        kp_l.append(kp); vp_l.append(vp); kip_l.append(kip); cp_l.append(cp)
        mkp_l.append(mk_p); mvp_l.append(mv_p)
        ks_l.append(ks_); vs_l.append(vs_); kis_l.append(kis); cs_l.append(cs)

    return (h_p, h_s,
            jnp.stack(kp_l), jnp.stack(vp_l), jnp.stack(kip_l), jnp.stack(cp_l),
            jnp.stack(mkp_l), jnp.stack(mvp_l),
            jnp.stack(ks_l), jnp.stack(vs_l), jnp.stack(kis_l), jnp.stack(cs_l))
```

```python
import functools
import math

import jax
import jax.numpy as jnp
from jax import lax
from jax.experimental import pallas as pl
from jax.experimental.pallas import tpu as pltpu

F32 = jnp.float32
BF16 = jnp.bfloat16

CHUNK = 64
CONV_WIDTH = 31
CONV_STATE = CONV_WIDTH - 1
HEAD_DIM = 128
N_KV_HEADS = 4
ROPE_DIM = HEAD_DIM // 4
ROPE_THETA = 500000.0
IDX_HEADS = 32
IDX_DIM = 128
IDX_SCALE = (IDX_HEADS ** -0.5) * (IDX_DIM ** -0.5)
TOPK_MAX = 256
ATTN_SCALE = HEAD_DIM ** -0.5
MEM_HEADS = 4
MEM_DIM = MEM_HEADS * HEAD_DIM
PEER_KEYS = 128
PEER_HEADS = 8
PEER_QDIM = 256
PEER_TOPK = 16
EPS = 1e-6

LANES = 128
VMEM_LIMIT_BYTES = 56 * 1024 * 1024

NEG_BIG = -1e30
INT_MIN = -(2 ** 31)

_NT = (((1,), (1,)), ((), ()))
_NN = (((1,), (0,)), ((), ()))
_TN = (((0,), (0,)), ((), ()))


def _params(*sem):
    return pltpu.CompilerParams(dimension_semantics=sem, vmem_limit_bytes=VMEM_LIMIT_BYTES)


def _pick(n, candidates):
    for c in candidates:
        if c <= n and n % c == 0:
            return c
    return n


def _rmsnorm_body(x_ref, g_ref, o_ref):
    x = x_ref[...]
    ms = jnp.mean(x * x, axis=-1, keepdims=True)
    o_ref[...] = (x * lax.rsqrt(ms + EPS) * g_ref[...]).astype(o_ref.dtype)


def _rmsnorm(x, g):
    n, d = x.shape
    tr = _pick(n, (256, 128, 64))
    return pl.pallas_call(
        _rmsnorm_body,
        grid=(n // tr,),
        in_specs=[pl.BlockSpec((tr, d), lambda i: (i, 0)),
                  pl.BlockSpec((1, d), lambda i: (0, 0))],
        out_specs=pl.BlockSpec((tr, d), lambda i: (i, 0)),
        out_shape=jax.ShapeDtypeStruct((n, d), BF16),
        compiler_params=_params("parallel"),
        name="rmsnorm",
    )(x, g.reshape(1, d))


def _mm_body(*refs, nk, dims, has_res):
    a_ref, b_ref = refs[0], refs[1]
    r_ref = refs[2] if has_res else None
    o_ref = refs[2 + has_res]
    p = lax.dot_general(a_ref[...], b_ref[...], dims, preferred_element_type=F32)
    if nk == 1:
        if has_res:
            p = r_ref[...] + p
        o_ref[...] = p.astype(o_ref.dtype)
        return
    acc_ref = refs[3 + has_res]
    k = pl.program_id(2)

    @pl.when(k == 0)
    def _():
        acc_ref[...] = p

    @pl.when(k > 0)
    def _():
        acc_ref[...] += p

    @pl.when(k == nk - 1)
    def _():
        r = acc_ref[...]
        if has_res:
            r = r_ref[...] + r
        o_ref[...] = r.astype(o_ref.dtype)


def _matmul(a, b, res=None, out_dtype=F32, trans_a=False, tm=None, tn=None, tk=None):
    if trans_a:
        kdim, m = a.shape
    else:
        m, kdim = a.shape
    n = b.shape[1]
    tm = tm or _pick(m, (1024, 512, 256, 128, 64))
    tn = tn or _pick(n, (768, 512, 384, 256, 128))
    tk = tk or kdim
    nk = kdim // tk
    if trans_a:
        a_spec = pl.BlockSpec((tk, tm), lambda i, j, k: (k, i))
        dims = _TN
    else:
        a_spec = pl.BlockSpec((tm, tk), lambda i, j, k: (i, k))
        dims = _NN
    in_specs = [a_spec, pl.BlockSpec((tk, tn), lambda i, j, k: (k, j))]
    args = [a, b]
    if res is not None:
        in_specs.append(pl.BlockSpec((tm, tn), lambda i, j, k: (i, j)))
        args.append(res)
    scratch = [pltpu.VMEM((tm, tn), F32)] if nk > 1 else []
    return pl.pallas_call(
        functools.partial(_mm_body, nk=nk, dims=dims, has_res=res is not None),
        grid=(m // tm, n // tn, nk),
        in_specs=in_specs,
        out_specs=pl.BlockSpec((tm, tn), lambda i, j, k: (i, j)),
        out_shape=jax.ShapeDtypeStruct((m, n), out_dtype),
        scratch_shapes=scratch,
        compiler_params=_params("parallel", "parallel", "arbitrary"),
        name="matmul",
    )(*args)


def _head_body(*refs, nh, has_norm, has_rope, scale, n_out):
    it = iter(refs)
    p_ref = next(it)
    g_ref = next(it) if has_norm else None
    if has_rope:
        c = next(it)[...]
        sa = next(it)[...]
        sb = next(it)[...]
    outs = [next(it) for _ in range(n_out)]
    for h in range(nh):
        sl = slice(h * HEAD_DIM, (h + 1) * HEAD_DIM)
        x = p_ref[:, sl]
        if has_norm:
            ms = jnp.mean(x * x, axis=-1, keepdims=True)
            x = x * lax.rsqrt(ms + EPS) * g_ref[...]
        if has_rope:
            x = x * c + pltpu.roll(x, LANES - ROPE_DIM // 2, 1) * sa + pltpu.roll(x, ROPE_DIM // 2, 1) * sb
        if scale != 1.0:
            x = x * scale
        for o in outs:
            o[:, sl] = x.astype(o.dtype)


def _head_post(p, col0, nh, t_len, out_dtypes, gain=None, rope=None, scale=1.0):
    n = p.shape[0]
    w = nh * HEAD_DIM
    assert col0 % w == 0
    tr = _pick(t_len, (256, 128, 64))
    tpb = t_len // tr
    in_specs = [pl.BlockSpec((tr, w), lambda i: (i, col0 // w))]
    args = [p]
    if gain is not None:
        in_specs.append(pl.BlockSpec((1, HEAD_DIM), lambda i: (0, 0)))
        args.append(gain.reshape(1, HEAD_DIM))
    if rope is not None:
        for tab in rope:
            in_specs.append(pl.BlockSpec((tr, LANES), lambda i: (i % tpb, 0)))
            args.append(tab)
    outs = pl.pallas_call(
        functools.partial(_head_body, nh=nh, has_norm=gain is not None, has_rope=rope is not None,
                          scale=scale, n_out=len(out_dtypes)),
        grid=(n // tr,),
        in_specs=in_specs,
        out_specs=[pl.BlockSpec((tr, w), lambda i: (i, 0)) for _ in out_dtypes],
        out_shape=[jax.ShapeDtypeStruct((n, w), dt) for dt in out_dtypes],
        compiler_params=_params("parallel"),
        name="head_post",
    )(*args)
    return outs


def _rope_tables(pos):
    half = ROPE_DIM // 2
    inv_freq = jnp.power(ROPE_THETA, -jnp.arange(half, dtype=F32) / half)
    ang = pos.astype(F32)[:, None] * inv_freq[None, :]
    cos, sin = jnp.cos(ang), jnp.sin(ang)
    t = pos.shape[0]
    zh = jnp.zeros((t, half), F32)
    zr = jnp.zeros((t, HEAD_DIM - ROPE_DIM), F32)
    c = jnp.concatenate([cos, cos, jnp.ones((t, HEAD_DIM - ROPE_DIM), F32)], axis=1)
    sa = jnp.concatenate([-sin, zh, zr], axis=1)
    sb = jnp.concatenate([zh, sin, zr], axis=1)
    return c, sa, sb


def _glu_body(a_ref, g_ref, o_ref):
    o_ref[...] = a_ref[...] * jax.nn.sigmoid(g_ref[...])


def _glu(p, col_a, col_g, c):
    n = p.shape[0]
    tr = _pick(n, (256, 128, 64))
    return pl.pallas_call(
        _glu_body,
        grid=(n // tr,),
        in_specs=[pl.BlockSpec((tr, c), lambda i: (i, col_a // c)),
                  pl.BlockSpec((tr, c), lambda i: (i, col_g // c))],
        out_specs=pl.BlockSpec((tr, c), lambda i: (i, 0)),
        out_shape=jax.ShapeDtypeStruct((n, c), F32),
        compiler_params=_params("parallel"),
        name="glu",
    )(p, p)


CONV_HALO = 32
CONV_CW = 256


def _conv_body(u_ref, prev_ref, w_ref, b_ref, g_ref, beta_ref, o_ref, win_ref, c_ref, *, tt, ch):
    @pl.when(pl.program_id(1) == 0)
    def _():
        win_ref[0:CONV_HALO, :] = prev_ref[0]

    win_ref[CONV_HALO:CONV_HALO + tt, :] = u_ref[0]
    base = CONV_HALO - CONV_STATE
    for cc in range(ch // CONV_CW):
        cs = slice(cc * CONV_CW, (cc + 1) * CONV_CW)
        acc = win_ref[base:base + tt, cs] * w_ref[0:1, cs]
        for j in range(1, CONV_WIDTH):
            acc = acc + win_ref[base + j:base + j + tt, cs] * w_ref[j:j + 1, cs]
        c_ref[:, cs] = acc + b_ref[:, cs]
    c = c_ref[...]
    mu = jnp.mean(c, axis=-1, keepdims=True)
    d = c - mu
    var = jnp.mean(d * d, axis=-1, keepdims=True)
    y = d * lax.rsqrt(var + EPS) * g_ref[...] + beta_ref[...]
    o_ref[0] = (y * jax.nn.sigmoid(y)).astype(o_ref.dtype)
    win_ref[0:CONV_HALO, :] = win_ref[tt:tt + CONV_HALO, :]


def _conv_module(u, prev, dw_w, dw_b, ln_g, ln_b):
    b, t, ch = u.shape
    tt = _pick(t, (256, 128, 64))
    prev_p = jnp.pad(prev, ((0, 0), (CONV_HALO - CONV_STATE, 0), (0, 0)))
    w_p = jnp.pad(dw_w, ((0, CONV_HALO - CONV_WIDTH), (0, 0)))
    row = lambda v: v.reshape(1, ch)
    return pl.pallas_call(
        functools.partial(_conv_body, tt=tt, ch=ch),
        grid=(b, t // tt),
        in_specs=[pl.BlockSpec((1, tt, ch), lambda i, j: (i, j, 0)),
                  pl.BlockSpec((1, CONV_HALO, ch), lambda i, j: (i, 0, 0)),
                  pl.BlockSpec((CONV_HALO, ch), lambda i, j: (0, 0)),
                  pl.BlockSpec((1, ch), lambda i, j: (0, 0)),
                  pl.BlockSpec((1, ch), lambda i, j: (0, 0)),
                  pl.BlockSpec((1, ch), lambda i, j: (0, 0))],
        out_specs=pl.BlockSpec((1, tt, ch), lambda i, j: (i, j, 0)),
        out_shape=jax.ShapeDtypeStruct((b, t, ch), BF16),
        scratch_shapes=[pltpu.VMEM((tt + CONV_HALO, ch), F32), pltpu.VMEM((tt, ch), F32)],
        compiler_params=_params("arbitrary", "arbitrary"),
        name="conv_module",
    )(u, prev_p, w_p, row(dw_b), row(ln_g), row(ln_b))


DSA_SC = 512


def _dsa_body(qs_ref, k_ref, v_ref, qi_ref, ki_ref, wi_ref, o_ref, key_ref, bias_ref, *,
              tq, nchunks, s_valid, topk, pos0, n_heads):
    sc = DSA_SC
    group = n_heads // N_KV_HEADS
    qpos = pos0 + pl.program_id(1) * tq + lax.broadcasted_iota(jnp.int32, (tq, 1), 0)
    chunk_shift = CHUNK.bit_length() - 1
    qchunk = qpos >> chunk_shift

    def score_chunk(c, carry):
        off = pl.multiple_of(c * sc, sc)
        ki_c = ki_ref[0, pl.ds(off, sc), :]
        acc = jnp.zeros((tq, sc), F32)
        for h in range(IDX_HEADS):
            s = lax.dot_general(qi_ref[0, :, h * IDX_DIM:(h + 1) * IDX_DIM], ki_c, _NT,
                                preferred_element_type=F32)
            acc = acc + wi_ref[0, :, h:h + 1] * jnp.maximum(s, 0.0)
        score = acc * IDX_SCALE
        spos = off + lax.broadcasted_iota(jnp.int32, (1, sc), 1)
        schunk = spos >> chunk_shift
        adm = jnp.logical_and(schunk <= qchunk, spos < s_valid)
        bits = pltpu.bitcast(score, jnp.int32)
        key = bits ^ ((bits >> 31) & 0x7FFFFFFF)
        key_ref[c] = jnp.where(adm, key, INT_MIN)
        return carry

    lax.fori_loop(0, nchunks, score_chunk, 0)

    def count_ge(cand):
        ge = (key_ref[...] >= cand[None]).astype(F32)
        return jnp.sum(jnp.sum(ge, axis=0), axis=-1, keepdims=True)

    zero = jnp.zeros((tq, 1), jnp.int32)
    ans = jnp.where(count_ge(zero) >= topk, zero, INT_MIN)

    def bit_step(i, ans):
        cand = ans | lax.shift_left(jnp.int32(1), 30 - i)
        return jnp.where(count_ge(cand) >= topk, cand, ans)

    ans = lax.fori_loop(0, 31, bit_step, ans)
    thr = jnp.maximum(ans, INT_MIN + 1)

    def bias_chunk(c, carry):
        bias_ref[c] = jnp.where(key_ref[c] >= thr, 0.0, NEG_BIG)
        return carry

    lax.fori_loop(0, nchunks, bias_chunk, 0)

    for hd in range(n_heads):
        g = hd // group
        q_h = qs_ref[0, :, hd * HEAD_DIM:(hd + 1) * HEAD_DIM]
        ks = slice(g * HEAD_DIM, (g + 1) * HEAD_DIM)

        def attn_chunk(c, carry, q_h=q_h, ks=ks):
            m, l, acc = carry
            off = pl.multiple_of(c * sc, sc)
            k_c = k_ref[0, pl.ds(off, sc), ks]
            v_c = v_ref[0, pl.ds(off, sc), ks]
            lg = lax.dot_general(q_h, k_c, _NT, preferred_element_type=F32) + bias_ref[c]
            m_new = jnp.maximum(m, jnp.max(lg, axis=-1, keepdims=True))
            alpha = jnp.exp(m - m_new)
            p = jnp.exp(lg - m_new)
            l = alpha * l + jnp.sum(p, axis=-1, keepdims=True)
            acc = alpha * acc + lax.dot_general(p.astype(BF16), v_c, _NN, preferred_element_type=F32)
            return m_new, l, acc

        init = (jnp.full((tq, 1), NEG_BIG, F32), jnp.zeros((tq, 1), F32), jnp.zeros((tq, HEAD_DIM), F32))
        m, l, acc = lax.fori_loop(0, nchunks, attn_chunk, init)
        o_ref[0, :, hd * HEAD_DIM:(hd + 1) * HEAD_DIM] = (acc / l).astype(o_ref.dtype)


def _dsa_attention(qs, k_all, v_all, qi, ki_all, p3, wi_col, s_valid, topk, pos0):
    b, t, qw = qs.shape
    s_pad = k_all.shape[1]
    n_heads = qw // HEAD_DIM
    tq = _pick(t, (128, 64))
    nchunks = s_pad // DSA_SC
    return pl.pallas_call(
        functools.partial(_dsa_body, tq=tq, nchunks=nchunks, s_valid=s_valid, topk=topk, pos0=pos0,
                          n_heads=n_heads),
        grid=(b, t // tq),
        in_specs=[pl.BlockSpec((1, tq, qw), lambda i, j: (i, j, 0)),
                  pl.BlockSpec((1, s_pad, N_KV_HEADS * HEAD_DIM), lambda i, j: (i, 0, 0)),
                  pl.BlockSpec((1, s_pad, N_KV_HEADS * HEAD_DIM), lambda i, j: (i, 0, 0)),
                  pl.BlockSpec((1, tq, IDX_HEADS * IDX_DIM), lambda i, j: (i, j, 0)),
                  pl.BlockSpec((1, s_pad, IDX_DIM), lambda i, j: (i, 0, 0)),
                  pl.BlockSpec((1, tq, LANES), lambda i, j: (i, j, wi_col // LANES))],
        out_specs=pl.BlockSpec((1, tq, qw), lambda i, j: (i, j, 0)),
        out_shape=jax.ShapeDtypeStruct((b, t, qw), BF16),
        scratch_shapes=[pltpu.VMEM((nchunks, tq, DSA_SC), jnp.int32),
                        pltpu.VMEM((nchunks, tq, DSA_SC), F32)],
        compiler_params=_params("parallel", "arbitrary"),
        name="dsa_attention",
    )(qs, k_all, v_all, qi, ki_all, p3)


def _memattn_body(q_ref, g_ref, mk_ref, mv_ref, o_ref):
    for h in range(MEM_HEADS):
        sl = slice(h * HEAD_DIM, (h + 1) * HEAD_DIM)
        x = q_ref[0, :, sl]
        ms = jnp.mean(x * x, axis=-1, keepdims=True)
        qn = (x * lax.rsqrt(ms + EPS) * g_ref[...] * ATTN_SCALE).astype(BF16)
        lg = lax.dot_general(qn, mk_ref[0, :, sl], _NT, preferred_element_type=F32)
        m = jnp.max(lg, axis=-1, keepdims=True)
        p = jnp.exp(lg - m)
        l = jnp.sum(p, axis=-1, keepdims=True)
        o = lax.dot_general(p.astype(BF16), mv_ref[0, :, sl], _NN, preferred_element_type=F32)
        o_ref[0, :, sl] = (o / l).astype(o_ref.dtype)


def _mem_attention(qm, gain, mk, mv):
    b, t, w = qm.shape
    m = mk.shape[1]
    tq = _pick(t, (512, 256, 128, 64))
    return pl.pallas_call(
        _memattn_body,
        grid=(b, t // tq),
        in_specs=[pl.BlockSpec((1, tq, w), lambda i, j: (i, j, 0)),
                  pl.BlockSpec((1, HEAD_DIM), lambda i, j: (0, 0)),
                  pl.BlockSpec((1, m, w), lambda i, j: (i, 0, 0)),
                  pl.BlockSpec((1, m, w), lambda i, j: (i, 0, 0))],
        out_specs=pl.BlockSpec((1, tq, w), lambda i, j: (i, j, 0)),
        out_shape=jax.ShapeDtypeStruct((b, t, w), BF16),
        compiler_params=_params("parallel", "parallel"),
        name="mem_attention",
    )(qm, gain.reshape(1, HEAD_DIM), mk, mv)


PEER_HALF = PEER_QDIM // 2
_PEER_PAIRS = [(a, b) for a in range(PEER_TOPK) for b in range(PEER_TOPK) if (a + 1) * (b + 1) <= PEER_TOPK]


def _route_body(pq_ref, k1_ref, k2_ref, s1_ref, s2_ref, e1_ref, e2_ref, tau_ref, v1_ref, v2_ref, c_ref, *, tt):
    row = lax.broadcasted_iota(jnp.int32, (PEER_KEYS, tt), 0)
    for h in range(PEER_HEADS):
        for kref, sref, vref, lo in ((k1_ref, s1_ref, v1_ref, 0), (k2_ref, s2_ref, v2_ref, PEER_HALF)):
            q = pq_ref[:, h * PEER_QDIM + lo:h * PEER_QDIM + lo + PEER_HALF]
            s = lax.dot_general(kref[h], q, _NT, preferred_element_type=F32)
            sref[h] = s
            cur = s
            for r in range(PEER_TOPK):
                m = jnp.max(cur, axis=0, keepdims=True)
                vref[r, h:h + 1, :] = m
                first = jnp.min(jnp.where(cur == m, row, PEER_KEYS), axis=0, keepdims=True)
                cur = jnp.where(row == first, -jnp.inf, cur)
    for p, (a, b) in enumerate(_PEER_PAIRS):
        c_ref[p] = v1_ref[a] + v2_ref[b]
    call = c_ref[...]

    def tau_step(p, tau):
        cp = c_ref[p]
        cnt = jnp.sum((call >= cp[None]).astype(F32), axis=0)
        return jnp.maximum(tau, jnp.where(cnt >= PEER_TOPK, cp, -jnp.inf))

    tau = lax.fori_loop(0, len(_PEER_PAIRS), tau_step, jnp.full((PEER_HEADS, tt), -jnp.inf, F32))
    tau_ref[...] = tau
    top = c_ref[0]
    z = jnp.sum(jnp.where(call >= tau[None], jnp.exp(call - top[None]), 0.0), axis=0)
    inv_z = 1.0 / z
    m1 = v1_ref[0]
    m2 = v2_ref[0]
    for h in range(PEER_HEADS):
        e1_ref[h] = jnp.exp(s1_ref[h] - m1[h:h + 1, :]) * inv_z[h:h + 1, :]
        e2_ref[h] = jnp.exp(s2_ref[h] - m2[h:h + 1, :])


def _peer_route(pq, k1, k2):
    n = pq.shape[0]
    tt = _pick(n, (256, 128))
    big = jax.ShapeDtypeStruct((PEER_HEADS, PEER_KEYS, n), F32)
    big_spec = pl.BlockSpec((PEER_HEADS, PEER_KEYS, tt), lambda i: (0, 0, i))
    kspec = pl.BlockSpec((PEER_HEADS, PEER_KEYS, PEER_HALF), lambda i: (0, 0, 0))
    return pl.pallas_call(
        functools.partial(_route_body, tt=tt),
        grid=(n // tt,),
        in_specs=[pl.BlockSpec((tt, PEER_HEADS * PEER_QDIM), lambda i: (i, 0)), kspec, kspec],
        out_specs=[big_spec, big_spec, big_spec, big_spec,
                   pl.BlockSpec((PEER_HEADS, tt), lambda i: (0, i))],
        out_shape=[big, big, big, big, jax.ShapeDtypeStruct((PEER_HEADS, n), F32)],
        scratch_shapes=[pltpu.VMEM((PEER_TOPK, PEER_HEADS, tt), F32),
                        pltpu.VMEM((PEER_TOPK, PEER_HEADS, tt), F32),
                        pltpu.VMEM((len(_PEER_PAIRS), PEER_HEADS, tt), F32)],
        compiler_params=_params("parallel"),
        name="peer_route",
    )(pq, k1, k2)


PEER_EB = 1024


def _coef_body(u_ref, x_ref, s1_ref, s2_ref, e1_ref, e2_ref, tau_ref, o_ref, *, tt):
    nb = PEER_EB // PEER_KEYS
    e = pl.program_id(1)
    act = lax.dot_general(u_ref[...], x_ref[...], _NT, preferred_element_type=F32)
    for j in range(nb):
        i1 = e * nb + j
        gate = jnp.zeros((PEER_KEYS, tt), F32)
        for h in range(PEER_HEADS):
            sm = s1_ref[h, pl.ds(i1, 1), :] + s2_ref[h]
            val = e1_ref[h, pl.ds(i1, 1), :] * e2_ref[h]
            gate = gate + jnp.where(sm >= tau_ref[h:h + 1, :], val, 0.0)
        a = act[j * PEER_KEYS:(j + 1) * PEER_KEYS, :]
        gelu = 0.5 * a * (1.0 + lax.erf(a * math.sqrt(0.5)))
        o_ref[j * PEER_KEYS:(j + 1) * PEER_KEYS, :] = (gate * gelu).astype(o_ref.dtype)


def _peer_coef(u_tab, xn, s1, s2, e1, e2, tau):
    ne, d = u_tab.shape
    n = xn.shape[0]
    tt = _pick(n, (512, 256, 128))
    big_spec = pl.BlockSpec((PEER_HEADS, PEER_KEYS, tt), lambda i, e: (0, 0, i))
    return pl.pallas_call(
        functools.partial(_coef_body, tt=tt),
        grid=(n // tt, ne // PEER_EB),
        in_specs=[pl.BlockSpec((PEER_EB, d), lambda i, e: (e, 0)),
                  pl.BlockSpec((tt, d), lambda i, e: (i, 0)),
                  big_spec, big_spec, big_spec, big_spec,
                  pl.BlockSpec((PEER_HEADS, tt), lambda i, e: (0, i))],
        out_specs=pl.BlockSpec((PEER_EB, tt), lambda i, e: (e, i)),
        out_shape=jax.ShapeDtypeStruct((ne, n), BF16),
        compiler_params=_params("parallel", "arbitrary"),
        name="peer_coef",
    )(u_tab, xn, s1, s2, e1, e2, tau)


def _round_up(x, m):
    return -(-x // m) * m


def _layer(x, pos0, conv_prev, k_past, v_past, ki_past, mem_k, mem_v, w):
    b, t, d = x.shape
    n = b * t
    ch = d // 2
    n_heads = (d - ch) // HEAD_DIM
    off = w["offsets"]
    pos = pos0 + jnp.arange(t, dtype=jnp.int32)
    rope = _rope_tables(pos)

    x2 = x.reshape(n, d)
    hn = _rmsnorm(x2, w["norm_mix_g"])
    p = _matmul(hn, w["w_in"])
    u = _glu(p, off["a"], off["g"], ch)
    (qs,) = _head_post(p, off["q"], n_heads, t, (BF16,), gain=w["q_norm_g"], rope=rope, scale=ATTN_SCALE)
    k32, k16 = _head_post(p, off["k"], N_KV_HEADS, t, (F32, BF16), gain=w["k_norm_g"], rope=rope)
    (qi,) = _head_post(p, off["qi"], IDX_HEADS, t, (BF16,), rope=rope)
    ki32, ki16 = _head_post(p, off["ki"], 1, t, (F32, BF16), rope=rope)
    kvw = N_KV_HEADS * HEAD_DIM
    v32 = p[:, off["v"]:off["v"] + kvw]
    v16 = v32.astype(BF16)

    u3 = u.reshape(b, t, ch)
    prev = jnp.zeros((b, CONV_STATE, ch), F32) if conv_prev is None else conv_prev
    conv_out = _conv_module(u3, prev, w["dw_w"], w["dw_b"], w["conv_ln_g"], w["conv_ln_b"])
    conv_new = u3[:, t - CONV_STATE:, :]

    k3, v3, ki3 = k16.reshape(b, t, kvw), v16.reshape(b, t, kvw), ki16.reshape(b, t, IDX_DIM)
    if k_past is not None:
        past = k_past.shape[1]
        k3 = jnp.concatenate([k_past.reshape(b, past, kvw).astype(BF16), k3], axis=1)
        v3 = jnp.concatenate([v_past.reshape(b, past, kvw).astype(BF16), v3], axis=1)
        ki3 = jnp.concatenate([ki_past.astype(BF16), ki3], axis=1)
    s_valid = k3.shape[1]
    s_pad = _round_up(s_valid, DSA_SC)
    if s_pad != s_valid:
        padw = ((0, 0), (0, s_pad - s_valid), (0, 0))
        k3, v3, ki3 = jnp.pad(k3, padw), jnp.pad(v3, padw), jnp.pad(ki3, padw)
    topk = min(TOPK_MAX, s_valid // 4)
    attn = _dsa_attention(qs.reshape(b, t, n_heads * HEAD_DIM), k3, v3, qi.reshape(b, t, IDX_HEADS * IDX_DIM),
                          ki3, p.reshape(b, t, p.shape[1]), off["wi"], s_valid, topk, pos0)

    mix = jnp.concatenate([conv_out.reshape(n, ch), attn.reshape(n, n_heads * HEAD_DIM)], axis=1)
    h1 = _matmul(mix, w["w_out"], res=x2)

    hn2 = _rmsnorm(h1, w["norm_mem_g"])
    qm = _matmul(hn2, w["w_q_mem"])
    om = _mem_attention(qm.reshape(b, t, MEM_DIM), w["mem_q_norm_g"], mem_k, mem_v)
    h2 = _matmul(om.reshape(n, MEM_DIM), w["w_o_mem"], res=h1)

    hn3 = _rmsnorm(h2, w["norm_ffn_g"])
    pq = _matmul(hn3, w["peer_wq"], out_dtype=BF16)
    s1, s2, e1, e2, tau = _peer_route(pq, w["peer_sub_k1"], w["peer_sub_k2"])
    coef_t = _peer_coef(w["peer_u"], hn3, s1, s2, e1, e2, tau)
    ne = coef_t.shape[0]
    y = _matmul(coef_t, w["peer_v"], res=h2, trans_a=True,
                tm=_pick(n, (1024, 512)), tn=_pick(d, (1024, 512)), tk=_pick(ne, (1024,)))

    return (y.reshape(b, t, d), k32.reshape(b, t, N_KV_HEADS, HEAD_DIM), v32.reshape(b, t, N_KV_HEADS, HEAD_DIM),
            ki32.reshape(b, t, IDX_DIM), conv_new)


def _mem_kv(mem, mem_norm_g, w_kv, mem_k_norm_g):
    b, m, d = mem.shape
    mn = _rmsnorm(mem.reshape(b * m, d), mem_norm_g)
    mkv = _matmul(mn, w_kv)
    mk32, mk16 = _head_post(mkv, 0, MEM_HEADS, m, (F32, BF16), gain=mem_k_norm_g)
    mv32 = mkv[:, MEM_DIM:]
    return mk32.reshape(b, m, MEM_DIM), mv32.reshape(b, m, MEM_DIM), mk16.reshape(b, m, MEM_DIM)


def _prep_layer_weights(l, d, norm_mix_g, w_in, dw_w, dw_b, conv_ln_g, conv_ln_b, q_norm_g, k_norm_g, w_out,
                        norm_mem_g, w_q_mem, mem_q_norm_g, w_o_mem, norm_ffn_g, peer_wq, peer_sub_k1,
                        peer_sub_k2, peer_u, peer_v):
    ch = d // 2
    nq = (d - ch)
    kvw = N_KV_HEADS * HEAD_DIM
    niq = IDX_HEADS * IDX_DIM
    src = {"a": 0, "g": ch, "q": 2 * ch}
    src["k"] = src["q"] + nq
    src["v"] = src["k"] + kvw
    src["qi"] = src["v"] + kvw
    src["ki"] = src["qi"] + niq
    src["wi"] = src["ki"] + IDX_DIM
    widths = {"qi": niq, "a": ch, "g": ch, "q": nq, "k": kvw, "v": kvw, "ki": IDX_DIM, "wi": IDX_HEADS}
    order = ["qi", "a", "g", "q", "k", "v", "ki", "wi"]
    wl = w_in[l]
    cols, offsets, o = [], {}, 0
    for name in order:
        offsets[name] = o
        cols.append(wl[:, src[name]:src[name] + widths[name]].astype(BF16))
        o += widths[name]
    pad = _round_up(o, LANES) - o
    cols.append(jnp.zeros((d, pad), BF16))
    return {
        "offsets": offsets,
        "w_in": jnp.concatenate(cols, axis=1),
        "norm_mix_g": norm_mix_g[l], "dw_w": dw_w[l], "dw_b": dw_b[l],
        "conv_ln_g": conv_ln_g[l], "conv_ln_b": conv_ln_b[l],
        "q_norm_g": q_norm_g[l], "k_norm_g": k_norm_g[l],
        "w_out": w_out[l].astype(BF16),
        "norm_mem_g": norm_mem_g[l], "w_q_mem": w_q_mem[l].astype(BF16), "mem_q_norm_g": mem_q_norm_g[l],
        "w_o_mem": w_o_mem[l].astype(BF16),
        "norm_ffn_g": norm_ffn_g[l], "peer_wq": peer_wq[l].astype(BF16),
        "peer_sub_k1": peer_sub_k1[l].astype(BF16), "peer_sub_k2": peer_sub_k2[l].astype(BF16),
        "peer_u": peer_u[l].astype(BF16), "peer_v": peer_v[l].astype(BF16),
    }


def kernel(x_prompt, x_sample, mem_prompt, cache_k, cache_v, cache_k_idx, state_conv, cache_mem_k, cache_mem_v, norm_mix_g, w_in, dw_w, dw_b, conv_ln_g, conv_ln_b, q_norm_g, k_norm_g, w_out, norm_mem_g, mem_norm_g, w_q_mem, w_k_mem, w_v_mem, mem_q_norm_g, mem_k_norm_g, w_o_mem, norm_ffn_g, peer_wq, peer_sub_k1, peer_sub_k2, peer_u, peer_v):
    depth = w_in.shape[0]
    d = x_prompt.shape[-1]
    past_len = cache_k.shape[2]
    h_p, h_s = x_prompt, x_sample
    outs = [[] for _ in range(10)]
    for l in range(depth):
        w = _prep_layer_weights(l, d, norm_mix_g, w_in, dw_w, dw_b, conv_ln_g, conv_ln_b, q_norm_g, k_norm_g,
                                w_out, norm_mem_g, w_q_mem, mem_q_norm_g, w_o_mem, norm_ffn_g, peer_wq,
                                peer_sub_k1, peer_sub_k2, peer_u, peer_v)
        w_kv = jnp.concatenate([w_k_mem[l], w_v_mem[l]], axis=1).astype(BF16)
        mk32, mv32, mk16 = _mem_kv(mem_prompt, mem_norm_g[l], w_kv, mem_k_norm_g[l])
        h_p, kp, vp, kip, cp = _layer(h_p, 0, None, None, None, None, mk16, mv32.astype(BF16), w)
        bs, ms = cache_mem_k.shape[1], cache_mem_k.shape[2]
        h_s, ks, vs, kis, cs = _layer(h_s, past_len, state_conv[l], cache_k[l], cache_v[l], cache_k_idx[l],
                                      cache_mem_k[l].reshape(bs, ms, MEM_DIM).astype(BF16),
                                      cache_mem_v[l].reshape(bs, ms, MEM_DIM).astype(BF16), w)
        bp, mp = mem_prompt.shape[0], mem_prompt.shape[1]
        for lst, val in zip(outs, (kp, vp, kip, cp, mk32.reshape(bp, mp, MEM_HEADS, HEAD_DIM),
                                   mv32.reshape(bp, mp, MEM_HEADS, HEAD_DIM), ks, vs, kis, cs)):
            lst.append(val)
    return (h_p, h_s) + tuple(jnp.stack(lst) for lst in outs)
```

```python
import functools
import math

import jax
import jax.numpy as jnp
from jax import lax
from jax.experimental import pallas as pl
from jax.experimental.pallas import tpu as pltpu

F32 = jnp.float32
BF16 = jnp.bfloat16

CHUNK = 64
CONV_WIDTH = 31
CONV_STATE = CONV_WIDTH - 1
HEAD_DIM = 128
N_KV_HEADS = 4
ROPE_DIM = HEAD_DIM // 4
ROPE_THETA = 500000.0
IDX_HEADS = 32
IDX_DIM = 128
IDX_SCALE = (IDX_HEADS ** -0.5) * (IDX_DIM ** -0.5)
TOPK_MAX = 256
ATTN_SCALE = HEAD_DIM ** -0.5
MEM_HEADS = 4
MEM_DIM = MEM_HEADS * HEAD_DIM
PEER_KEYS = 128
PEER_HEADS = 8
PEER_QDIM = 256
PEER_TOPK = 16
EPS = 1e-6

LANES = 128
VMEM_LIMIT_BYTES = 56 * 1024 * 1024

NEG_BIG = -1e30
INT_MIN = -(2 ** 31)

_NT = (((1,), (1,)), ((), ()))
_NN = (((1,), (0,)), ((), ()))
_TN = (((0,), (0,)), ((), ()))


def _params(*sem):
    return pltpu.CompilerParams(dimension_semantics=sem, vmem_limit_bytes=VMEM_LIMIT_BYTES)


def _pick(n, candidates):
    for c in candidates:
        if c <= n and n % c == 0:
            return c
    return n


def _rmsnorm_body(x_ref, g_ref, o_ref):
    x = x_ref[...]
    ms = jnp.mean(x * x, axis=-1, keepdims=True)
    o_ref[...] = (x * lax.rsqrt(ms + EPS) * g_ref[...]).astype(o_ref.dtype)


def _rmsnorm(x, g):
    n, d = x.shape
    tr = _pick(n, (256, 128, 64))
    return pl.pallas_call(
        _rmsnorm_body,
        grid=(n // tr,),
        in_specs=[pl.BlockSpec((tr, d), lambda i: (i, 0)),
                  pl.BlockSpec((1, d), lambda i: (0, 0))],
        out_specs=pl.BlockSpec((tr, d), lambda i: (i, 0)),
        out_shape=jax.ShapeDtypeStruct((n, d), BF16),
        compiler_params=_params("parallel"),
        name="rmsnorm",
    )(x, g.reshape(1, d))


def _mm_body(*refs, nk, dims, has_res):
    a_ref, b_ref = refs[0], refs[1]
    r_ref = refs[2] if has_res else None
    o_ref = refs[2 + has_res]
    p = lax.dot_general(a_ref[...], b_ref[...], dims, preferred_element_type=F32)
    if nk == 1:
        if has_res:
            p = r_ref[...] + p
        o_ref[...] = p.astype(o_ref.dtype)
        return
    acc_ref = refs[3 + has_res]
    k = pl.program_id(2)

    @pl.when(k == 0)
    def _():
        acc_ref[...] = p

    @pl.when(k > 0)
    def _():
        acc_ref[...] += p

    @pl.when(k == nk - 1)
    def _():
        r = acc_ref[...]
        if has_res:
            r = r_ref[...] + r
        o_ref[...] = r.astype(o_ref.dtype)


def _matmul(a, b, res=None, out_dtype=F32, trans_a=False, tm=None, tn=None, tk=None):
    if trans_a:
        kdim, m = a.shape
    else:
        m, kdim = a.shape
    n = b.shape[1]
    tm = tm or _pick(m, (1024, 512, 256, 128, 64))
    tn = tn or _pick(n, (768, 512, 384, 256, 128))
    tk = tk or kdim
    nk = kdim // tk
    if trans_a:
        a_spec = pl.BlockSpec((tk, tm), lambda i, j, k: (k, i))
        dims = _TN
    else:
        a_spec = pl.BlockSpec((tm, tk), lambda i, j, k: (i, k))
        dims = _NN
    in_specs = [a_spec, pl.BlockSpec((tk, tn), lambda i, j, k: (k, j))]
    args = [a, b]
    if res is not None:
        in_specs.append(pl.BlockSpec((tm, tn), lambda i, j, k: (i, j)))
        args.append(res)
    scratch = [pltpu.VMEM((tm, tn), F32)] if nk > 1 else []
    return pl.pallas_call(
        functools.partial(_mm_body, nk=nk, dims=dims, has_res=res is not None),
        grid=(m // tm, n // tn, nk),
        in_specs=in_specs,
        out_specs=pl.BlockSpec((tm, tn), lambda i, j, k: (i, j)),
        out_shape=jax.ShapeDtypeStruct((m, n), out_dtype),
        scratch_shapes=scratch,
        compiler_params=_params("parallel", "parallel", "arbitrary"),
        name="matmul",
    )(*args)


def _head_body(*refs, nh, has_norm, has_rope, scale, n_out):
    it = iter(refs)
    p_ref = next(it)
    g_ref = next(it) if has_norm else None
    if has_rope:
        c = next(it)[...]
        sa = next(it)[...]
        sb = next(it)[...]
    outs = [next(it) for _ in range(n_out)]
    for h in range(nh):
        sl = slice(h * HEAD_DIM, (h + 1) * HEAD_DIM)
        x = p_ref[:, sl]
        if has_norm:
            ms = jnp.mean(x * x, axis=-1, keepdims=True)
            x = x * lax.rsqrt(ms + EPS) * g_ref[...]
        if has_rope:
            x = x * c + pltpu.roll(x, LANES - ROPE_DIM // 2, 1) * sa + pltpu.roll(x, ROPE_DIM // 2, 1) * sb
        if scale != 1.0:
            x = x * scale
        for o in outs:
            o[:, sl] = x.astype(o.dtype)


def _head_post(p, col0, nh, t_len, out_dtypes, gain=None, rope=None, scale=1.0):
    n = p.shape[0]
    w = nh * HEAD_DIM
    assert col0 % w == 0
    tr = _pick(t_len, (256, 128, 64))
    tpb = t_len // tr
    in_specs = [pl.BlockSpec((tr, w), lambda i: (i, col0 // w))]
    args = [p]
    if gain is not None:
        in_specs.append(pl.BlockSpec((1, HEAD_DIM), lambda i: (0, 0)))
        args.append(gain.reshape(1, HEAD_DIM))
    if rope is not None:
        for tab in rope:
            in_specs.append(pl.BlockSpec((tr, LANES), lambda i: (i % tpb, 0)))
            args.append(tab)
    outs = pl.pallas_call(
        functools.partial(_head_body, nh=nh, has_norm=gain is not None, has_rope=rope is not None,
                          scale=scale, n_out=len(out_dtypes)),
        grid=(n // tr,),
        in_specs=in_specs,
        out_specs=[pl.BlockSpec((tr, w), lambda i: (i, 0)) for _ in out_dtypes],
        out_shape=[jax.ShapeDtypeStruct((n, w), dt) for dt in out_dtypes],
        compiler_params=_params("parallel"),
        name="head_post",
    )(*args)
    return outs


def _rope_tables(pos):
    half = ROPE_DIM // 2
    inv_freq = jnp.power(ROPE_THETA, -jnp.arange(half, dtype=F32) / half)
    ang = pos.astype(F32)[:, None] * inv_freq[None, :]
    cos, sin = jnp.cos(ang), jnp.sin(ang)
    t = pos.shape[0]
    zh = jnp.zeros((t, half), F32)
    zr = jnp.zeros((t, HEAD_DIM - ROPE_DIM), F32)
    c = jnp.concatenate([cos, cos, jnp.ones((t, HEAD_DIM - ROPE_DIM), F32)], axis=1)
    sa = jnp.concatenate([-sin, zh, zr], axis=1)
    sb = jnp.concatenate([zh, sin, zr], axis=1)
    return c, sa, sb


def _glu_body(a_ref, g_ref, o_ref):
    o_ref[...] = a_ref[...] * jax.nn.sigmoid(g_ref[...])


def _glu(p, col_a, col_g, c):
    n = p.shape[0]
    tr = _pick(n, (256, 128, 64))
    return pl.pallas_call(
        _glu_body,
        grid=(n // tr,),
        in_specs=[pl.BlockSpec((tr, c), lambda i: (i, col_a // c)),
                  pl.BlockSpec((tr, c), lambda i: (i, col_g // c))],
        out_specs=pl.BlockSpec((tr, c), lambda i: (i, 0)),
        out_shape=jax.ShapeDtypeStruct((n, c), F32),
        compiler_params=_params("parallel"),
        name="glu",
    )(p, p)


CONV_HALO = 32
CONV_CW = 256


def _conv_body(u_ref, prev_ref, w_ref, b_ref, g_ref, beta_ref, o_ref, win_ref, c_ref, *, tt, ch):
    @pl.when(pl.program_id(1) == 0)
    def _():
        win_ref[0:CONV_HALO, :] = prev_ref[0]

    win_ref[CONV_HALO:CONV_HALO + tt, :] = u_ref[0]
    base = CONV_HALO - CONV_STATE
    for cc in range(ch // CONV_CW):
        cs = slice(cc * CONV_CW, (cc + 1) * CONV_CW)
        acc = win_ref[base:base + tt, cs] * w_ref[0:1, cs]
        for j in range(1, CONV_WIDTH):
            acc = acc + win_ref[base + j:base + j + tt, cs] * w_ref[j:j + 1, cs]
        c_ref[:, cs] = acc + b_ref[:, cs]
    c = c_ref[...]
    mu = jnp.mean(c, axis=-1, keepdims=True)
    d = c - mu
    var = jnp.mean(d * d, axis=-1, keepdims=True)
    y = d * lax.rsqrt(var + EPS) * g_ref[...] + beta_ref[...]
    o_ref[0] = (y * jax.nn.sigmoid(y)).astype(o_ref.dtype)
    win_ref[0:CONV_HALO, :] = win_ref[tt:tt + CONV_HALO, :]


def _conv_module(u, prev, dw_w, dw_b, ln_g, ln_b):
    b, t, ch = u.shape
    tt = _pick(t, (256, 128, 64))
    prev_p = jnp.pad(prev, ((0, 0), (CONV_HALO - CONV_STATE, 0), (0, 0)))
    w_p = jnp.pad(dw_w, ((0, CONV_HALO - CONV_WIDTH), (0, 0)))
    row = lambda v: v.reshape(1, ch)
    return pl.pallas_call(
        functools.partial(_conv_body, tt=tt, ch=ch),
        grid=(b, t // tt),
        in_specs=[pl.BlockSpec((1, tt, ch), lambda i, j: (i, j, 0)),
                  pl.BlockSpec((1, CONV_HALO, ch), lambda i, j: (i, 0, 0)),
                  pl.BlockSpec((CONV_HALO, ch), lambda i, j: (0, 0)),
                  pl.BlockSpec((1, ch), lambda i, j: (0, 0)),
                  pl.BlockSpec((1, ch), lambda i, j: (0, 0)),
                  pl.BlockSpec((1, ch), lambda i, j: (0, 0))],
        out_specs=pl.BlockSpec((1, tt, ch), lambda i, j: (i, j, 0)),
        out_shape=jax.ShapeDtypeStruct((b, t, ch), BF16),
        scratch_shapes=[pltpu.VMEM((tt + CONV_HALO, ch), F32), pltpu.VMEM((tt, ch), F32)],
        compiler_params=_params("arbitrary", "arbitrary"),
        name="conv_module",
    )(u, prev_p, w_p, row(dw_b), row(ln_g), row(ln_b))


DSA_SC = 512


def _dsa_body(qs_ref, k_ref, v_ref, qi_ref, ki_ref, wi_ref, o_ref, key_ref, bias_ref, thr_ref, qst_ref,
              mx_ref, acc_ref, *, tq, nchunks, buckets, s_valid, topk, pos0, n_heads):
    sc = DSA_SC
    group = n_heads // N_KV_HEADS
    rows = group * tq
    q_lo = pos0 + pl.program_id(1) * tq
    qpos = q_lo + lax.broadcasted_iota(jnp.int32, (tq, 1), 0)
    chunk_shift = CHUNK.bit_length() - 1
    qchunk = qpos >> chunk_shift
    k_end = jnp.minimum((((q_lo + tq - 1) >> chunk_shift) + 1) << chunk_shift, s_valid)
    nact = jnp.minimum((k_end + sc - 1) >> (sc.bit_length() - 1), nchunks)
    nbucket = jnp.int32(buckets[-1])
    for bsz in reversed(buckets[:-1]):
        nbucket = jnp.where(nact <= bsz, bsz, nbucket)

    def score_chunk(c, carry):
        off = pl.multiple_of(c * sc, sc)
        ki_c = ki_ref[0, pl.ds(off, sc), :]
        acc = jnp.zeros((tq, sc), F32)
        for h in range(IDX_HEADS):
            s = lax.dot_general(qi_ref[0, :, h * IDX_DIM:(h + 1) * IDX_DIM], ki_c, _NT,
                                preferred_element_type=F32)
            acc = acc + wi_ref[0, :, h:h + 1] * jnp.maximum(s, 0.0)
        score = acc * IDX_SCALE
        spos = off + lax.broadcasted_iota(jnp.int32, (1, sc), 1)
        schunk = spos >> chunk_shift
        adm = jnp.logical_and(schunk <= qchunk, spos < s_valid)
        bits = pltpu.bitcast(score, jnp.int32)
        key = bits ^ ((bits >> 31) & 0x7FFFFFFF)
        key_ref[c] = jnp.where(adm, key, INT_MIN)
        return carry

    lax.fori_loop(0, nact, score_chunk, 0)

    def fill_chunk(c, carry):
        key_ref[c] = jnp.full((tq, sc), INT_MIN, jnp.int32)
        return carry

    lax.fori_loop(nact, nbucket, fill_chunk, 0)

    for bsz in buckets:
        @pl.when(nbucket == bsz)
        def _(bsz=bsz):
            def count_ge(cand):
                ge = (key_ref[0:bsz] >= cand[None]).astype(F32)
                return jnp.sum(jnp.sum(ge, axis=0), axis=-1, keepdims=True)

            zero = jnp.zeros((tq, 1), jnp.int32)
            ans = jnp.where(count_ge(zero) >= topk, zero, INT_MIN)

            def bit_step(i, ans):
                cand = ans | lax.shift_left(jnp.int32(1), 30 - i)
                return jnp.where(count_ge(cand) >= topk, cand, ans)

            ans = lax.fori_loop(0, 31, bit_step, ans)
            thr_ref[...] = jnp.maximum(ans, INT_MIN + 1)

    thr = thr_ref[...]

    def bias_chunk(c, carry):
        bias_ref[c] = jnp.where(key_ref[c] >= thr, 0.0, NEG_BIG)
        return carry

    lax.fori_loop(0, nact, bias_chunk, 0)

    for hd in range(n_heads):
        qst_ref[hd * tq:(hd + 1) * tq, :] = qs_ref[0, :, hd * HEAD_DIM:(hd + 1) * HEAD_DIM]
    ntile = sc // LANES

    def masked_logits(c, g):
        off = pl.multiple_of(c * sc, sc)
        k_c = k_ref[0, pl.ds(off, sc), g * HEAD_DIM:(g + 1) * HEAD_DIM]
        lg = lax.dot_general(qst_ref[g * rows:(g + 1) * rows, :], k_c, _NT, preferred_element_type=F32)
        return (lg.reshape(group, tq, sc) + bias_ref[c][None]).reshape(rows, sc)

    mx_ref[...] = jnp.full(mx_ref.shape, NEG_BIG, F32)

    def max_chunk(c, carry):
        for g in range(N_KV_HEADS):
            rs = slice(g * rows, (g + 1) * rows)
            lg = masked_logits(c, g)
            mx = mx_ref[rs, :]
            for t in range(ntile):
                mx = jnp.maximum(mx, lg[:, t * LANES:(t + 1) * LANES])
            mx_ref[rs, :] = mx
        return carry

    lax.fori_loop(0, nact, max_chunk, 0)
    mx_ref[...] = jnp.broadcast_to(jnp.max(mx_ref[...], axis=-1, keepdims=True), mx_ref.shape)

    acc_ref[...] = jnp.zeros(acc_ref.shape, F32)
    ones = jnp.ones((sc, HEAD_DIM), BF16)

    def attn_chunk(c, carry):
        off = pl.multiple_of(c * sc, sc)
        for g in range(N_KV_HEADS):
            rs = slice(g * rows, (g + 1) * rows)
            lg = masked_logits(c, g)
            mb = mx_ref[rs, :]
            p = jnp.concatenate([jnp.exp(lg[:, t * LANES:(t + 1) * LANES] - mb) for t in range(ntile)], axis=1)
            v_ext = jnp.concatenate([v_ref[0, pl.ds(off, sc), g * HEAD_DIM:(g + 1) * HEAD_DIM], ones], axis=1)
            acc_ref[rs, :] += lax.dot_general(p.astype(BF16), v_ext, _NN, preferred_element_type=F32)
        return carry

    lax.fori_loop(0, nact, attn_chunk, 0)
    for hd in range(n_heads):
        hs = slice(hd * tq, (hd + 1) * tq)
        o = acc_ref[hs, 0:HEAD_DIM] / acc_ref[hs, HEAD_DIM:2 * HEAD_DIM]
        o_ref[0, :, hd * HEAD_DIM:(hd + 1) * HEAD_DIM] = o.astype(o_ref.dtype)


def _dsa_attention(qs, k_all, v_all, qi, ki_all, p3, wi_col, s_valid, topk, pos0):
    b, t, qw = qs.shape
    s_pad = k_all.shape[1]
    n_heads = qw // HEAD_DIM
    tq = _pick(t, (128, 64))
    nchunks = s_pad // DSA_SC
    if pos0 + tq >= s_valid:
        buckets = (nchunks,)
    else:
        buckets = tuple(sorted({min(1 << e, nchunks) for e in range(nchunks.bit_length() + 1)}))
    return pl.pallas_call(
        functools.partial(_dsa_body, tq=tq, nchunks=nchunks, buckets=buckets, s_valid=s_valid, topk=topk,
                          pos0=pos0, n_heads=n_heads),
        grid=(b, t // tq),
        in_specs=[pl.BlockSpec((1, tq, qw), lambda i, j: (i, j, 0)),
                  pl.BlockSpec((1, s_pad, N_KV_HEADS * HEAD_DIM), lambda i, j: (i, 0, 0)),
                  pl.BlockSpec((1, s_pad, N_KV_HEADS * HEAD_DIM), lambda i, j: (i, 0, 0)),
                  pl.BlockSpec((1, tq, IDX_HEADS * IDX_DIM), lambda i, j: (i, j, 0)),
                  pl.BlockSpec((1, s_pad, IDX_DIM), lambda i, j: (i, 0, 0)),
                  pl.BlockSpec((1, tq, LANES), lambda i, j: (i, j, wi_col // LANES))],
        out_specs=pl.BlockSpec((1, tq, qw), lambda i, j: (i, j, 0)),
        out_shape=jax.ShapeDtypeStruct((b, t, qw), BF16),
        scratch_shapes=[pltpu.VMEM((nchunks, tq, DSA_SC), jnp.int32),
                        pltpu.VMEM((nchunks, tq, DSA_SC), F32),
                        pltpu.VMEM((tq, 1), jnp.int32),
                        pltpu.VMEM((n_heads * tq, HEAD_DIM), BF16),
                        pltpu.VMEM((n_heads * tq, LANES), F32),
                        pltpu.VMEM((n_heads * tq, 2 * HEAD_DIM), F32)],
        compiler_params=_params("parallel", "arbitrary"),
        name="dsa_attention",
    )(qs, k_all, v_all, qi, ki_all, p3)


def _memattn_body(q_ref, g_ref, mk_ref, mv_ref, o_ref):
    for h in range(MEM_HEADS):
        sl = slice(h * HEAD_DIM, (h + 1) * HEAD_DIM)
        x = q_ref[0, :, sl]
        ms = jnp.mean(x * x, axis=-1, keepdims=True)
        qn = (x * lax.rsqrt(ms + EPS) * g_ref[...] * ATTN_SCALE).astype(BF16)
        lg = lax.dot_general(qn, mk_ref[0, :, sl], _NT, preferred_element_type=F32)
        m = jnp.max(lg, axis=-1, keepdims=True)
        p = jnp.exp(lg - m)
        l = jnp.sum(p, axis=-1, keepdims=True)
        o = lax.dot_general(p.astype(BF16), mv_ref[0, :, sl], _NN, preferred_element_type=F32)
        o_ref[0, :, sl] = (o / l).astype(o_ref.dtype)


def _mem_attention(qm, gain, mk, mv):
    b, t, w = qm.shape
    m = mk.shape[1]
    tq = _pick(t, (512, 256, 128, 64))
    return pl.pallas_call(
        _memattn_body,
        grid=(b, t // tq),
        in_specs=[pl.BlockSpec((1, tq, w), lambda i, j: (i, j, 0)),
                  pl.BlockSpec((1, HEAD_DIM), lambda i, j: (0, 0)),
                  pl.BlockSpec((1, m, w), lambda i, j: (i, 0, 0)),
                  pl.BlockSpec((1, m, w), lambda i, j: (i, 0, 0))],
        out_specs=pl.BlockSpec((1, tq, w), lambda i, j: (i, j, 0)),
        out_shape=jax.ShapeDtypeStruct((b, t, w), BF16),
        compiler_params=_params("parallel", "parallel"),
        name="mem_attention",
    )(qm, gain.reshape(1, HEAD_DIM), mk, mv)


PEER_HALF = PEER_QDIM // 2
_PEER_PAIRS = [(a, b) for a in range(PEER_TOPK) for b in range(PEER_TOPK) if (a + 1) * (b + 1) <= PEER_TOPK]


def _route_body(pq_ref, k1_ref, k2_ref, s1_ref, s2_ref, e1_ref, e2_ref, tau_ref, v1_ref, v2_ref, c_ref, *, tt):
    row = lax.broadcasted_iota(jnp.int32, (PEER_KEYS, tt), 0)
    for h in range(PEER_HEADS):
        for kref, sref, vref, lo in ((k1_ref, s1_ref, v1_ref, 0), (k2_ref, s2_ref, v2_ref, PEER_HALF)):
            q = pq_ref[:, h * PEER_QDIM + lo:h * PEER_QDIM + lo + PEER_HALF]
            s = lax.dot_general(kref[h], q, _NT, preferred_element_type=F32)
            sref[h] = s
            cur = s
            for r in range(PEER_TOPK):
                m = jnp.max(cur, axis=0, keepdims=True)
                vref[r, h:h + 1, :] = m
                first = jnp.min(jnp.where(cur == m, row, PEER_KEYS), axis=0, keepdims=True)
                cur = jnp.where(row == first, -jnp.inf, cur)
    for p, (a, b) in enumerate(_PEER_PAIRS):
        c_ref[p] = v1_ref[a] + v2_ref[b]
    call = c_ref[...]

    def tau_step(p, tau):
        cp = c_ref[p]
        cnt = jnp.sum((call >= cp[None]).astype(F32), axis=0)
        return jnp.maximum(tau, jnp.where(cnt >= PEER_TOPK, cp, -jnp.inf))

    tau = lax.fori_loop(0, len(_PEER_PAIRS), tau_step, jnp.full((PEER_HEADS, tt), -jnp.inf, F32))
    tau_ref[...] = tau
    top = c_ref[0]
    z = jnp.sum(jnp.where(call >= tau[None], jnp.exp(call - top[None]), 0.0), axis=0)
    inv_z = 1.0 / z
    m1 = v1_ref[0]
    m2 = v2_ref[0]
    for h in range(PEER_HEADS):
        e1_ref[h] = jnp.exp(s1_ref[h] - m1[h:h + 1, :]) * inv_z[h:h + 1, :]
        e2_ref[h] = jnp.exp(s2_ref[h] - m2[h:h + 1, :])


def _peer_route(pq, k1, k2):
    n = pq.shape[0]
    tt = _pick(n, (256, 128))
    big = jax.ShapeDtypeStruct((PEER_HEADS, PEER_KEYS, n), F32)
    big_spec = pl.BlockSpec((PEER_HEADS, PEER_KEYS, tt), lambda i: (0, 0, i))
    kspec = pl.BlockSpec((PEER_HEADS, PEER_KEYS, PEER_HALF), lambda i: (0, 0, 0))
    return pl.pallas_call(
        functools.partial(_route_body, tt=tt),
        grid=(n // tt,),
        in_specs=[pl.BlockSpec((tt, PEER_HEADS * PEER_QDIM), lambda i: (i, 0)), kspec, kspec],
        out_specs=[big_spec, big_spec, big_spec, big_spec,
                   pl.BlockSpec((PEER_HEADS, tt), lambda i: (0, i))],
        out_shape=[big, big, big, big, jax.ShapeDtypeStruct((PEER_HEADS, n), F32)],
        scratch_shapes=[pltpu.VMEM((PEER_TOPK, PEER_HEADS, tt), F32),
                        pltpu.VMEM((PEER_TOPK, PEER_HEADS, tt), F32),
                        pltpu.VMEM((len(_PEER_PAIRS), PEER_HEADS, tt), F32)],
        compiler_params=_params("parallel"),
        name="peer_route",
    )(pq, k1, k2)


PEER_EB = 1024


def _coef_body(u_ref, x_ref, s1_ref, s2_ref, e1_ref, e2_ref, tau_ref, o_ref, *, tt):
    nb = PEER_EB // PEER_KEYS
    e = pl.program_id(1)
    act = lax.dot_general(u_ref[...], x_ref[...], _NT, preferred_element_type=F32)
    for j in range(nb):
        i1 = e * nb + j
        gate = jnp.zeros((PEER_KEYS, tt), F32)
        for h in range(PEER_HEADS):
            sm = s1_ref[h, pl.ds(i1, 1), :] + s2_ref[h]
            val = e1_ref[h, pl.ds(i1, 1), :] * e2_ref[h]
            gate = gate + jnp.where(sm >= tau_ref[h:h + 1, :], val, 0.0)
        a = act[j * PEER_KEYS:(j + 1) * PEER_KEYS, :]
        gelu = 0.5 * a * (1.0 + lax.erf(a * math.sqrt(0.5)))
        o_ref[j * PEER_KEYS:(j + 1) * PEER_KEYS, :] = (gate * gelu).astype(o_ref.dtype)


def _peer_coef(u_tab, xn, s1, s2, e1, e2, tau):
    ne, d = u_tab.shape
    n = xn.shape[0]
    tt = _pick(n, (512, 256, 128))
    big_spec = pl.BlockSpec((PEER_HEADS, PEER_KEYS, tt), lambda i, e: (0, 0, i))
    return pl.pallas_call(
        functools.partial(_coef_body, tt=tt),
        grid=(n // tt, ne // PEER_EB),
        in_specs=[pl.BlockSpec((PEER_EB, d), lambda i, e: (e, 0)),
                  pl.BlockSpec((tt, d), lambda i, e: (i, 0)),
                  big_spec, big_spec, big_spec, big_spec,
                  pl.BlockSpec((PEER_HEADS, tt), lambda i, e: (0, i))],
        out_specs=pl.BlockSpec((PEER_EB, tt), lambda i, e: (e, i)),
        out_shape=jax.ShapeDtypeStruct((ne, n), BF16),
        compiler_params=_params("parallel", "arbitrary"),
        name="peer_coef",
    )(u_tab, xn, s1, s2, e1, e2, tau)


def _round_up(x, m):
    return -(-x // m) * m


def _layer(x, pos0, conv_prev, k_past, v_past, ki_past, mem_k, mem_v, w):
    b, t, d = x.shape
    n = b * t
    ch = d // 2
    n_heads = (d - ch) // HEAD_DIM
    off = w["offsets"]
    pos = pos0 + jnp.arange(t, dtype=jnp.int32)
    rope = _rope_tables(pos)

    x2 = x.reshape(n, d)
    hn = _rmsnorm(x2, w["norm_mix_g"])
    p = _matmul(hn, w["w_in"])
    u = _glu(p, off["a"], off["g"], ch)
    (qs,) = _head_post(p, off["q"], n_heads, t, (BF16,), gain=w["q_norm_g"], rope=rope, scale=ATTN_SCALE)
    k32, k16 = _head_post(p, off["k"], N_KV_HEADS, t, (F32, BF16), gain=w["k_norm_g"], rope=rope)
    (qi,) = _head_post(p, off["qi"], IDX_HEADS, t, (BF16,), rope=rope)
    ki32, ki16 = _head_post(p, off["ki"], 1, t, (F32, BF16), rope=rope)
    kvw = N_KV_HEADS * HEAD_DIM
    v32 = p[:, off["v"]:off["v"] + kvw]
    v16 = v32.astype(BF16)

    u3 = u.reshape(b, t, ch)
    prev = jnp.zeros((b, CONV_STATE, ch), F32) if conv_prev is None else conv_prev
    conv_out = _conv_module(u3, prev, w["dw_w"], w["dw_b"], w["conv_ln_g"], w["conv_ln_b"])
    conv_new = u3[:, t - CONV_STATE:, :]

    k3, v3, ki3 = k16.reshape(b, t, kvw), v16.reshape(b, t, kvw), ki16.reshape(b, t, IDX_DIM)
    if k_past is not None:
        past = k_past.shape[1]
        k3 = jnp.concatenate([k_past.reshape(b, past, kvw).astype(BF16), k3], axis=1)
        v3 = jnp.concatenate([v_past.reshape(b, past, kvw).astype(BF16), v3], axis=1)
        ki3 = jnp.concatenate([ki_past.astype(BF16), ki3], axis=1)
    s_valid = k3.shape[1]
    s_pad = _round_up(s_valid, DSA_SC)
    if s_pad != s_valid:
        padw = ((0, 0), (0, s_pad - s_valid), (0, 0))
        k3, v3, ki3 = jnp.pad(k3, padw), jnp.pad(v3, padw), jnp.pad(ki3, padw)
    topk = min(TOPK_MAX, s_valid // 4)
    attn = _dsa_attention(qs.reshape(b, t, n_heads * HEAD_DIM), k3, v3, qi.reshape(b, t, IDX_HEADS * IDX_DIM),
                          ki3, p.reshape(b, t, p.shape[1]), off["wi"], s_valid, topk, pos0)

    mix = jnp.concatenate([conv_out.reshape(n, ch), attn.reshape(n, n_heads * HEAD_DIM)], axis=1)
    h1 = _matmul(mix, w["w_out"], res=x2)

    hn2 = _rmsnorm(h1, w["norm_mem_g"])
    qm = _matmul(hn2, w["w_q_mem"])
    om = _mem_attention(qm.reshape(b, t, MEM_DIM), w["mem_q_norm_g"], mem_k, mem_v)
    h2 = _matmul(om.reshape(n, MEM_DIM), w["w_o_mem"], res=h1)

    hn3 = _rmsnorm(h2, w["norm_ffn_g"])
    pq = _matmul(hn3, w["peer_wq"], out_dtype=BF16)
    s1, s2, e1, e2, tau = _peer_route(pq, w["peer_sub_k1"], w["peer_sub_k2"])
    coef_t = _peer_coef(w["peer_u"], hn3, s1, s2, e1, e2, tau)
    ne = coef_t.shape[0]
    y = _matmul(coef_t, w["peer_v"], res=h2, trans_a=True,
                tm=_pick(n, (1024, 512)), tn=_pick(d, (1024, 512)), tk=_pick(ne, (1024,)))

    return (y.reshape(b, t, d), k32.reshape(b, t, N_KV_HEADS, HEAD_DIM), v32.reshape(b, t, N_KV_HEADS, HEAD_DIM),
            ki32.reshape(b, t, IDX_DIM), conv_new)


def _mem_kv(mem, mem_norm_g, w_kv, mem_k_norm_g):
    b, m, d = mem.shape
    mn = _rmsnorm(mem.reshape(b * m, d), mem_norm_g)
    mkv = _matmul(mn, w_kv)
    mk32, mk16 = _head_post(mkv, 0, MEM_HEADS, m, (F32, BF16), gain=mem_k_norm_g)
    mv32 = mkv[:, MEM_DIM:]
    return mk32.reshape(b, m, MEM_DIM), mv32.reshape(b, m, MEM_DIM), mk16.reshape(b, m, MEM_DIM)


def _prep_layer_weights(l, d, norm_mix_g, w_in, dw_w, dw_b, conv_ln_g, conv_ln_b, q_norm_g, k_norm_g, w_out,
                        norm_mem_g, w_q_mem, mem_q_norm_g, w_o_mem, norm_ffn_g, peer_wq, peer_sub_k1,
                        peer_sub_k2, peer_u, peer_v):
    ch = d // 2
    nq = (d - ch)
    kvw = N_KV_HEADS * HEAD_DIM
    niq = IDX_HEADS * IDX_DIM
    src = {"a": 0, "g": ch, "q": 2 * ch}
    src["k"] = src["q"] + nq
    src["v"] = src["k"] + kvw
    src["qi"] = src["v"] + kvw
    src["ki"] = src["qi"] + niq
    src["wi"] = src["ki"] + IDX_DIM
    widths = {"qi": niq, "a": ch, "g": ch, "q": nq, "k": kvw, "v": kvw, "ki": IDX_DIM, "wi": IDX_HEADS}
    order = ["qi", "a", "g", "q", "k", "v", "ki", "wi"]
    wl = w_in[l]
    cols, offsets, o = [], {}, 0
    for name in order:
        offsets[name] = o
        cols.append(wl[:, src[name]:src[name] + widths[name]].astype(BF16))
        o += widths[name]
    pad = _round_up(o, LANES) - o
    cols.append(jnp.zeros((d, pad), BF16))
    return {
        "offsets": offsets,
        "w_in": jnp.concatenate(cols, axis=1),
        "norm_mix_g": norm_mix_g[l], "dw_w": dw_w[l], "dw_b": dw_b[l],
        "conv_ln_g": conv_ln_g[l], "conv_ln_b": conv_ln_b[l],
        "q_norm_g": q_norm_g[l], "k_norm_g": k_norm_g[l],
        "w_out": w_out[l].astype(BF16),
        "norm_mem_g": norm_mem_g[l], "w_q_mem": w_q_mem[l].astype(BF16), "mem_q_norm_g": mem_q_norm_g[l],
        "w_o_mem": w_o_mem[l].astype(BF16),
        "norm_ffn_g": norm_ffn_g[l], "peer_wq": peer_wq[l].astype(BF16),
        "peer_sub_k1": peer_sub_k1[l].astype(BF16), "peer_sub_k2": peer_sub_k2[l].astype(BF16),
        "peer_u": peer_u[l].astype(BF16), "peer_v": peer_v[l].astype(BF16),
    }


def kernel(x_prompt, x_sample, mem_prompt, cache_k, cache_v, cache_k_idx, state_conv, cache_mem_k, cache_mem_v, norm_mix_g, w_in, dw_w, dw_b, conv_ln_g, conv_ln_b, q_norm_g, k_norm_g, w_out, norm_mem_g, mem_norm_g, w_q_mem, w_k_mem, w_v_mem, mem_q_norm_g, mem_k_norm_g, w_o_mem, norm_ffn_g, peer_wq, peer_sub_k1, peer_sub_k2, peer_u, peer_v):
    depth = w_in.shape[0]
    d = x_prompt.shape[-1]
    past_len = cache_k.shape[2]
    h_p, h_s = x_prompt, x_sample
    outs = [[] for _ in range(10)]
    for l in range(depth):
        w = _prep_layer_weights(l, d, norm_mix_g, w_in, dw_w, dw_b, conv_ln_g, conv_ln_b, q_norm_g, k_norm_g,
                                w_out, norm_mem_g, w_q_mem, mem_q_norm_g, w_o_mem, norm_ffn_g, peer_wq,
                                peer_sub_k1, peer_sub_k2, peer_u, peer_v)
        w_kv = jnp.concatenate([w_k_mem[l], w_v_mem[l]], axis=1).astype(BF16)
        mk32, mv32, mk16 = _mem_kv(mem_prompt, mem_norm_g[l], w_kv, mem_k_norm_g[l])
        h_p, kp, vp, kip, cp = _layer(h_p, 0, None, None, None, None, mk16, mv32.astype(BF16), w)
        bs, ms = cache_mem_k.shape[1], cache_mem_k.shape[2]
        h_s, ks, vs, kis, cs = _layer(h_s, past_len, state_conv[l], cache_k[l], cache_v[l], cache_k_idx[l],
                                      cache_mem_k[l].reshape(bs, ms, MEM_DIM).astype(BF16),
                                      cache_mem_v[l].reshape(bs, ms, MEM_DIM).astype(BF16), w)
        bp, mp = mem_prompt.shape[0], mem_prompt.shape[1]
        for lst, val in zip(outs, (kp, vp, kip, cp, mk32.reshape(bp, mp, MEM_HEADS, HEAD_DIM),
                                   mv32.reshape(bp, mp, MEM_HEADS, HEAD_DIM), ks, vs, kis, cs)):
            lst.append(val)
    return (h_p, h_s) + tuple(jnp.stack(lst) for lst in outs)
```

```python
import functools
import math

import jax
import jax.numpy as jnp
from jax import lax
from jax.experimental import pallas as pl
from jax.experimental.pallas import tpu as pltpu

F32 = jnp.float32
BF16 = jnp.bfloat16

CHUNK = 64
CONV_WIDTH = 31
CONV_STATE = CONV_WIDTH - 1
HEAD_DIM = 128
N_KV_HEADS = 4
ROPE_DIM = HEAD_DIM // 4
ROPE_THETA = 500000.0
IDX_HEADS = 32
IDX_DIM = 128
IDX_SCALE = (IDX_HEADS ** -0.5) * (IDX_DIM ** -0.5)
TOPK_MAX = 256
ATTN_SCALE = HEAD_DIM ** -0.5
MEM_HEADS = 4
MEM_DIM = MEM_HEADS * HEAD_DIM
PEER_KEYS = 128
PEER_HEADS = 8
PEER_QDIM = 256
PEER_TOPK = 16
EPS = 1e-6

LANES = 128
VMEM_LIMIT_BYTES = 56 * 1024 * 1024

NEG_BIG = -1e30
INT_MIN = -(2 ** 31)

_NT = (((1,), (1,)), ((), ()))
_NN = (((1,), (0,)), ((), ()))
_TN = (((0,), (0,)), ((), ()))


def _params(*sem):
    return pltpu.CompilerParams(dimension_semantics=sem, vmem_limit_bytes=VMEM_LIMIT_BYTES)


def _pick(n, candidates):
    for c in candidates:
        if c <= n and n % c == 0:
            return c
    return n


def _rmsnorm_body(x_ref, g_ref, o_ref):
    x = x_ref[...]
    ms = jnp.mean(x * x, axis=-1, keepdims=True)
    o_ref[...] = (x * lax.rsqrt(ms + EPS) * g_ref[...]).astype(o_ref.dtype)


def _rmsnorm(x, g):
    n, d = x.shape
    tr = _pick(n, (256, 128, 64))
    return pl.pallas_call(
        _rmsnorm_body,
        grid=(n // tr,),
        in_specs=[pl.BlockSpec((tr, d), lambda i: (i, 0)),
                  pl.BlockSpec((1, d), lambda i: (0, 0))],
        out_specs=pl.BlockSpec((tr, d), lambda i: (i, 0)),
        out_shape=jax.ShapeDtypeStruct((n, d), BF16),
        compiler_params=_params("parallel"),
        name="rmsnorm",
    )(x, g.reshape(1, d))


def _mm_body(*refs, nk, dims, has_res):
    a_ref, b_ref = refs[0], refs[1]
    r_ref = refs[2] if has_res else None
    o_ref = refs[2 + has_res]
    p = lax.dot_general(a_ref[...], b_ref[...], dims, preferred_element_type=F32)
    if nk == 1:
        if has_res:
            p = r_ref[...] + p
        o_ref[...] = p.astype(o_ref.dtype)
        return
    acc_ref = refs[3 + has_res]
    k = pl.program_id(2)

    @pl.when(k == 0)
    def _():
        acc_ref[...] = p

    @pl.when(k > 0)
    def _():
        acc_ref[...] += p

    @pl.when(k == nk - 1)
    def _():
        r = acc_ref[...]
        if has_res:
            r = r_ref[...] + r
        o_ref[...] = r.astype(o_ref.dtype)


def _matmul(a, b, res=None, out_dtype=F32, trans_a=False, tm=None, tn=None, tk=None):
    if trans_a:
        kdim, m = a.shape
    else:
        m, kdim = a.shape
    n = b.shape[1]
    tm = tm or _pick(m, (1024, 512, 256, 128, 64))
    tn = tn or _pick(n, (768, 512, 384, 256, 128))
    tk = tk or kdim
    nk = kdim // tk
    if trans_a:
        a_spec = pl.BlockSpec((tk, tm), lambda i, j, k: (k, i))
        dims = _TN
    else:
        a_spec = pl.BlockSpec((tm, tk), lambda i, j, k: (i, k))
        dims = _NN
    in_specs = [a_spec, pl.BlockSpec((tk, tn), lambda i, j, k: (k, j))]
    args = [a, b]
    if res is not None:
        in_specs.append(pl.BlockSpec((tm, tn), lambda i, j, k: (i, j)))
        args.append(res)
    scratch = [pltpu.VMEM((tm, tn), F32)] if nk > 1 else []
    return pl.pallas_call(
        functools.partial(_mm_body, nk=nk, dims=dims, has_res=res is not None),
        grid=(m // tm, n // tn, nk),
        in_specs=in_specs,
        out_specs=pl.BlockSpec((tm, tn), lambda i, j, k: (i, j)),
        out_shape=jax.ShapeDtypeStruct((m, n), out_dtype),
        scratch_shapes=scratch,
        compiler_params=_params("parallel", "parallel", "arbitrary"),
        name="matmul",
    )(*args)


def _head_body(*refs, nh, has_norm, has_rope, scale, n_out):
    it = iter(refs)
    p_ref = next(it)
    g_ref = next(it) if has_norm else None
    if has_rope:
        c = next(it)[...]
        sa = next(it)[...]
        sb = next(it)[...]
    outs = [next(it) for _ in range(n_out)]
    for h in range(nh):
        sl = slice(h * HEAD_DIM, (h + 1) * HEAD_DIM)
        x = p_ref[:, sl]
        if has_norm:
            ms = jnp.mean(x * x, axis=-1, keepdims=True)
            x = x * lax.rsqrt(ms + EPS) * g_ref[...]
        if has_rope:
            x = x * c + pltpu.roll(x, LANES - ROPE_DIM // 2, 1) * sa + pltpu.roll(x, ROPE_DIM // 2, 1) * sb
        if scale != 1.0:
            x = x * scale
        for o in outs:
            o[:, sl] = x.astype(o.dtype)


def _head_post(p, col0, nh, t_len, out_dtypes, gain=None, rope=None, scale=1.0):
    n = p.shape[0]
    w = nh * HEAD_DIM
    assert col0 % w == 0
    tr = _pick(t_len, (256, 128, 64))
    tpb = t_len // tr
    in_specs = [pl.BlockSpec((tr, w), lambda i: (i, col0 // w))]
    args = [p]
    if gain is not None:
        in_specs.append(pl.BlockSpec((1, HEAD_DIM), lambda i: (0, 0)))
        args.append(gain.reshape(1, HEAD_DIM))
    if rope is not None:
        for tab in rope:
            in_specs.append(pl.BlockSpec((tr, LANES), lambda i: (i % tpb, 0)))
            args.append(tab)
    outs = pl.pallas_call(
        functools.partial(_head_body, nh=nh, has_norm=gain is not None, has_rope=rope is not None,
                          scale=scale, n_out=len(out_dtypes)),
        grid=(n // tr,),
        in_specs=in_specs,
        out_specs=[pl.BlockSpec((tr, w), lambda i: (i, 0)) for _ in out_dtypes],
        out_shape=[jax.ShapeDtypeStruct((n, w), dt) for dt in out_dtypes],
        compiler_params=_params("parallel"),
        name="head_post",
    )(*args)
    return outs


def _rope_tables(pos):
    half = ROPE_DIM // 2
    inv_freq = jnp.power(ROPE_THETA, -jnp.arange(half, dtype=F32) / half)
    ang = pos.astype(F32)[:, None] * inv_freq[None, :]
    cos, sin = jnp.cos(ang), jnp.sin(ang)
    t = pos.shape[0]
    zh = jnp.zeros((t, half), F32)
    zr = jnp.zeros((t, HEAD_DIM - ROPE_DIM), F32)
    c = jnp.concatenate([cos, cos, jnp.ones((t, HEAD_DIM - ROPE_DIM), F32)], axis=1)
    sa = jnp.concatenate([-sin, zh, zr], axis=1)
    sb = jnp.concatenate([zh, sin, zr], axis=1)
    return c, sa, sb


def _glu_body(a_ref, g_ref, o_ref):
    o_ref[...] = a_ref[...] * jax.nn.sigmoid(g_ref[...])


def _glu(p, col_a, col_g, c):
    n = p.shape[0]
    tr = _pick(n, (256, 128, 64))
    return pl.pallas_call(
        _glu_body,
        grid=(n // tr,),
        in_specs=[pl.BlockSpec((tr, c), lambda i: (i, col_a // c)),
                  pl.BlockSpec((tr, c), lambda i: (i, col_g // c))],
        out_specs=pl.BlockSpec((tr, c), lambda i: (i, 0)),
        out_shape=jax.ShapeDtypeStruct((n, c), F32),
        compiler_params=_params("parallel"),
        name="glu",
    )(p, p)


CONV_HALO = 32
CONV_CW = 256


def _conv_body(u_ref, prev_ref, w_ref, b_ref, g_ref, beta_ref, o_ref, win_ref, c_ref, *, tt, ch):
    @pl.when(pl.program_id(1) == 0)
    def _():
        win_ref[0:CONV_HALO, :] = prev_ref[0]

    win_ref[CONV_HALO:CONV_HALO + tt, :] = u_ref[0]
    base = CONV_HALO - CONV_STATE
    for cc in range(ch // CONV_CW):
        cs = slice(cc * CONV_CW, (cc + 1) * CONV_CW)
        acc = win_ref[base:base + tt, cs] * w_ref[0:1, cs]
        for j in range(1, CONV_WIDTH):
            acc = acc + win_ref[base + j:base + j + tt, cs] * w_ref[j:j + 1, cs]
        c_ref[:, cs] = acc + b_ref[:, cs]
    c = c_ref[...]
    mu = jnp.mean(c, axis=-1, keepdims=True)
    d = c - mu
    var = jnp.mean(d * d, axis=-1, keepdims=True)
    y = d * lax.rsqrt(var + EPS) * g_ref[...] + beta_ref[...]
    o_ref[0] = (y * jax.nn.sigmoid(y)).astype(o_ref.dtype)
    win_ref[0:CONV_HALO, :] = win_ref[tt:tt + CONV_HALO, :]


def _conv_module(u, prev, dw_w, dw_b, ln_g, ln_b):
    b, t, ch = u.shape
    tt = _pick(t, (256, 128, 64))
    prev_p = jnp.pad(prev, ((0, 0), (CONV_HALO - CONV_STATE, 0), (0, 0)))
    w_p = jnp.pad(dw_w, ((0, CONV_HALO - CONV_WIDTH), (0, 0)))
    row = lambda v: v.reshape(1, ch)
    return pl.pallas_call(
        functools.partial(_conv_body, tt=tt, ch=ch),
        grid=(b, t // tt),
        in_specs=[pl.BlockSpec((1, tt, ch), lambda i, j: (i, j, 0)),
                  pl.BlockSpec((1, CONV_HALO, ch), lambda i, j: (i, 0, 0)),
                  pl.BlockSpec((CONV_HALO, ch), lambda i, j: (0, 0)),
                  pl.BlockSpec((1, ch), lambda i, j: (0, 0)),
                  pl.BlockSpec((1, ch), lambda i, j: (0, 0)),
                  pl.BlockSpec((1, ch), lambda i, j: (0, 0))],
        out_specs=pl.BlockSpec((1, tt, ch), lambda i, j: (i, j, 0)),
        out_shape=jax.ShapeDtypeStruct((b, t, ch), BF16),
        scratch_shapes=[pltpu.VMEM((tt + CONV_HALO, ch), F32), pltpu.VMEM((tt, ch), F32)],
        compiler_params=_params("arbitrary", "arbitrary"),
        name="conv_module",
    )(u, prev_p, w_p, row(dw_b), row(ln_g), row(ln_b))


DSA_SC = 512


def _dsa_body(qs_ref, k_ref, v_ref, qi_ref, ki_ref, wi_ref, o_ref, key_ref, bias_ref, thr_ref, qst_ref,
              mx_ref, acc_ref, *, tq, nchunks, buckets, s_valid, topk, pos0, n_heads):
    sc = DSA_SC
    group = n_heads // N_KV_HEADS
    rows = group * tq
    q_lo = pos0 + pl.program_id(1) * tq
    qpos = q_lo + lax.broadcasted_iota(jnp.int32, (tq, 1), 0)
    chunk_shift = CHUNK.bit_length() - 1
    qchunk = qpos >> chunk_shift
    k_end = jnp.minimum((((q_lo + tq - 1) >> chunk_shift) + 1) << chunk_shift, s_valid)
    nact = jnp.minimum((k_end + sc - 1) >> (sc.bit_length() - 1), nchunks)
    nbucket = jnp.int32(buckets[-1])
    for bsz in reversed(buckets[:-1]):
        nbucket = jnp.where(nact <= bsz, bsz, nbucket)

    def score_chunk(c, carry):
        off = pl.multiple_of(c * sc, sc)
        ki_c = ki_ref[0, pl.ds(off, sc), :]
        acc = jnp.zeros((tq, sc), F32)
        for h in range(IDX_HEADS):
            s = lax.dot_general(qi_ref[0, :, h * IDX_DIM:(h + 1) * IDX_DIM], ki_c, _NT,
                                preferred_element_type=F32)
            acc = acc + wi_ref[0, :, h:h + 1] * jnp.maximum(s, 0.0)
        score = acc * IDX_SCALE
        spos = off + lax.broadcasted_iota(jnp.int32, (1, sc), 1)
        schunk = spos >> chunk_shift
        adm = jnp.logical_and(schunk <= qchunk, spos < s_valid)
        bits = pltpu.bitcast(score, jnp.int32)
        key = bits ^ ((bits >> 31) & 0x7FFFFFFF)
        key_ref[c] = jnp.where(adm, key, INT_MIN)
        return carry

    lax.fori_loop(0, nact, score_chunk, 0)

    def fill_chunk(c, carry):
        key_ref[c] = jnp.full((tq, sc), INT_MIN, jnp.int32)
        return carry

    lax.fori_loop(nact, nbucket, fill_chunk, 0)

    for bsz in buckets:
        @pl.when(nbucket == bsz)
        def _(bsz=bsz):
            def count_ge(cand):
                ge = (key_ref[0:bsz] >= cand[None]).astype(F32)
                return jnp.sum(jnp.sum(ge, axis=0), axis=-1, keepdims=True)

            zero = jnp.zeros((tq, 1), jnp.int32)
            ans = jnp.where(count_ge(zero) >= topk, zero, INT_MIN)

            def bit_step(i, ans):
                cand = ans | lax.shift_left(jnp.int32(1), 30 - i)
                return jnp.where(count_ge(cand) >= topk, cand, ans)

            ans = lax.fori_loop(0, 31, bit_step, ans)
            thr_ref[...] = jnp.maximum(ans, INT_MIN + 1)

    thr = thr_ref[...]

    def bias_chunk(c, carry):
        bias_ref[c] = jnp.where(key_ref[c] >= thr, 0.0, NEG_BIG)
        return carry

    lax.fori_loop(0, nact, bias_chunk, 0)

    for hd in range(n_heads):
        qst_ref[hd * tq:(hd + 1) * tq, :] = qs_ref[0, :, hd * HEAD_DIM:(hd + 1) * HEAD_DIM]
    ntile = sc // LANES

    def masked_logits(c, g):
        off = pl.multiple_of(c * sc, sc)
        k_c = k_ref[0, pl.ds(off, sc), g * HEAD_DIM:(g + 1) * HEAD_DIM]
        lg = lax.dot_general(qst_ref[g * rows:(g + 1) * rows, :], k_c, _NT, preferred_element_type=F32)
        return (lg.reshape(group, tq, sc) + bias_ref[c][None]).reshape(rows, sc)

    mx_ref[...] = jnp.full(mx_ref.shape, NEG_BIG, F32)

    def max_chunk(c, carry):
        for g in range(N_KV_HEADS):
            rs = slice(g * rows, (g + 1) * rows)
            lg = masked_logits(c, g)
            mx = mx_ref[rs, :]
            for t in range(ntile):
                mx = jnp.maximum(mx, lg[:, t * LANES:(t + 1) * LANES])
            mx_ref[rs, :] = mx
        return carry

    lax.fori_loop(0, nact, max_chunk, 0)
    mx_ref[...] = jnp.broadcast_to(jnp.max(mx_ref[...], axis=-1, keepdims=True), mx_ref.shape)

    acc_ref[...] = jnp.zeros(acc_ref.shape, F32)
    ones = jnp.ones((sc, HEAD_DIM), BF16)

    def attn_chunk(c, carry):
        off = pl.multiple_of(c * sc, sc)
        for g in range(N_KV_HEADS):
            rs = slice(g * rows, (g + 1) * rows)
            lg = masked_logits(c, g)
            mb = mx_ref[rs, :]
            p = jnp.concatenate([jnp.exp(lg[:, t * LANES:(t + 1) * LANES] - mb) for t in range(ntile)], axis=1)
            v_ext = jnp.concatenate([v_ref[0, pl.ds(off, sc), g * HEAD_DIM:(g + 1) * HEAD_DIM], ones], axis=1)
            acc_ref[rs, :] += lax.dot_general(p.astype(BF16), v_ext, _NN, preferred_element_type=F32)
        return carry

    lax.fori_loop(0, nact, attn_chunk, 0)
    for hd in range(n_heads):
        hs = slice(hd * tq, (hd + 1) * tq)
        o = acc_ref[hs, 0:HEAD_DIM] / acc_ref[hs, HEAD_DIM:2 * HEAD_DIM]
        o_ref[0, :, hd * HEAD_DIM:(hd + 1) * HEAD_DIM] = o.astype(o_ref.dtype)


def _dsa_attention(qs, k_all, v_all, qi, ki_all, p3, wi_col, s_valid, topk, pos0):
    b, t, qw = qs.shape
    s_pad = k_all.shape[1]
    n_heads = qw // HEAD_DIM
    tq = _pick(t, (128, 64))
    nchunks = s_pad // DSA_SC
    if pos0 + tq >= s_valid:
        buckets = (nchunks,)
    else:
        buckets = tuple(sorted({min(1 << e, nchunks) for e in range(nchunks.bit_length() + 1)}))
    return pl.pallas_call(
        functools.partial(_dsa_body, tq=tq, nchunks=nchunks, buckets=buckets, s_valid=s_valid, topk=topk,
                          pos0=pos0, n_heads=n_heads),
        grid=(b, t // tq),
        in_specs=[pl.BlockSpec((1, tq, qw), lambda i, j: (i, j, 0)),
                  pl.BlockSpec((1, s_pad, N_KV_HEADS * HEAD_DIM), lambda i, j: (i, 0, 0)),
                  pl.BlockSpec((1, s_pad, N_KV_HEADS * HEAD_DIM), lambda i, j: (i, 0, 0)),
                  pl.BlockSpec((1, tq, IDX_HEADS * IDX_DIM), lambda i, j: (i, j, 0)),
                  pl.BlockSpec((1, s_pad, IDX_DIM), lambda i, j: (i, 0, 0)),
                  pl.BlockSpec((1, tq, LANES), lambda i, j: (i, j, wi_col // LANES))],
        out_specs=pl.BlockSpec((1, tq, qw), lambda i, j: (i, j, 0)),
        out_shape=jax.ShapeDtypeStruct((b, t, qw), BF16),
        scratch_shapes=[pltpu.VMEM((nchunks, tq, DSA_SC), jnp.int32),
                        pltpu.VMEM((nchunks, tq, DSA_SC), F32),
                        pltpu.VMEM((tq, 1), jnp.int32),
                        pltpu.VMEM((n_heads * tq, HEAD_DIM), BF16),
                        pltpu.VMEM((n_heads * tq, LANES), F32),
                        pltpu.VMEM((n_heads * tq, 2 * HEAD_DIM), F32)],
        compiler_params=_params("parallel", "arbitrary"),
        name="dsa_attention",
    )(qs, k_all, v_all, qi, ki_all, p3)


def _memattn_body(q_ref, g_ref, mk_ref, mv_ref, o_ref):
    for h in range(MEM_HEADS):
        sl = slice(h * HEAD_DIM, (h + 1) * HEAD_DIM)
        x = q_ref[0, :, sl]
        ms = jnp.mean(x * x, axis=-1, keepdims=True)
        qn = (x * lax.rsqrt(ms + EPS) * g_ref[...] * ATTN_SCALE).astype(BF16)
        lg = lax.dot_general(qn, mk_ref[0, :, sl], _NT, preferred_element_type=F32)
        m = jnp.max(lg, axis=-1, keepdims=True)
        p = jnp.exp(lg - m)
        l = jnp.sum(p, axis=-1, keepdims=True)
        o = lax.dot_general(p.astype(BF16), mv_ref[0, :, sl], _NN, preferred_element_type=F32)
        o_ref[0, :, sl] = (o / l).astype(o_ref.dtype)


def _mem_attention(qm, gain, mk, mv):
    b, t, w = qm.shape
    m = mk.shape[1]
    tq = _pick(t, (512, 256, 128, 64))
    return pl.pallas_call(
        _memattn_body,
        grid=(b, t // tq),
        in_specs=[pl.BlockSpec((1, tq, w), lambda i, j: (i, j, 0)),
                  pl.BlockSpec((1, HEAD_DIM), lambda i, j: (0, 0)),
                  pl.BlockSpec((1, m, w), lambda i, j: (i, 0, 0)),
                  pl.BlockSpec((1, m, w), lambda i, j: (i, 0, 0))],
        out_specs=pl.BlockSpec((1, tq, w), lambda i, j: (i, j, 0)),
        out_shape=jax.ShapeDtypeStruct((b, t, w), BF16),
        compiler_params=_params("parallel", "parallel"),
        name="mem_attention",
    )(qm, gain.reshape(1, HEAD_DIM), mk, mv)


PEER_HALF = PEER_QDIM // 2
_PEER_PAIRS = [(a, b) for a in range(PEER_TOPK) for b in range(PEER_TOPK) if (a + 1) * (b + 1) <= PEER_TOPK]


def _route_body(pq_ref, k1_ref, k2_ref, r2_ref, e2_ref, na_ref, e1_ref, rk1_ref, v1_ref, v2_ref, c_ref, *, tt):
    row = lax.broadcasted_iota(jnp.int32, (PEER_KEYS, tt), 0)
    for h in range(PEER_HEADS):
        for table, (kref, vref, lo) in enumerate(((k1_ref, v1_ref, 0), (k2_ref, v2_ref, PEER_HALF))):
            q = pq_ref[:, h * PEER_QDIM + lo:h * PEER_QDIM + lo + PEER_HALF]
            s = lax.dot_general(kref[h], q, _NT, preferred_element_type=F32)
            cur = s
            rank = jnp.full((PEER_KEYS, tt), float(PEER_TOPK), F32)
            for r in range(PEER_TOPK):
                m = jnp.max(cur, axis=0, keepdims=True)
                vref[r, h:h + 1, :] = m
                first = jnp.min(jnp.where(cur == m, row, PEER_KEYS), axis=0, keepdims=True)
                hit = row == first
                cur = jnp.where(hit, -jnp.inf, cur)
                rank = jnp.where(hit, float(r), rank)
            ex = jnp.exp(s - vref[0, h:h + 1, :])
            if table == 0:
                rk1_ref[h] = rank
                e1_ref[h] = ex
            else:
                r2_ref[h] = rank.astype(r2_ref.dtype)
                e2_ref[h] = ex.astype(e2_ref.dtype)
    for p, (a, b) in enumerate(_PEER_PAIRS):
        c_ref[p] = v1_ref[a] + v2_ref[b]
    call = c_ref[...]

    def tau_step(p, tau):
        cp = c_ref[p]
        cnt = jnp.sum((call >= cp[None]).astype(F32), axis=0)
        return jnp.maximum(tau, jnp.where(cnt >= PEER_TOPK, cp, -jnp.inf))

    tau = lax.fori_loop(0, len(_PEER_PAIRS), tau_step, jnp.full((PEER_HEADS, tt), -jnp.inf, F32))
    zero = jnp.zeros((PEER_HEADS, tt), F32)
    cnt_gt = [zero] * PEER_TOPK
    cnt_eq = [zero] * PEER_TOPK
    for p, (a, b) in enumerate(_PEER_PAIRS):
        cnt_gt[a] = cnt_gt[a] + (call[p] > tau).astype(F32)
        cnt_eq[a] = cnt_eq[a] + (call[p] == tau).astype(F32)
    rem = float(PEER_TOPK) - sum(cnt_gt)
    n_of = []
    for a in range(PEER_TOPK):
        take = jnp.minimum(cnt_eq[a], rem)
        rem = rem - take
        n_of.append(cnt_gt[a] + take)
    z = zero
    for p, (a, b) in enumerate(_PEER_PAIRS):
        z = z + jnp.where(n_of[a] > float(b), jnp.exp(call[p] - call[0]), 0.0)
    inv_z = 1.0 / z
    for h in range(PEER_HEADS):
        rk = rk1_ref[h]
        na = jnp.zeros((PEER_KEYS, tt), F32)
        for a in range(PEER_TOPK):
            na = jnp.where(rk == float(a), n_of[a][h:h + 1, :], na)
        na_ref[h] = na
        e1_ref[h] = e1_ref[h] * inv_z[h:h + 1, :]


def _peer_route(pq, k1, k2):
    n = pq.shape[0]
    tt = _pick(n, (256, 128))
    shape = (PEER_HEADS, PEER_KEYS, n)
    big_spec = pl.BlockSpec((PEER_HEADS, PEER_KEYS, tt), lambda i: (0, 0, i))
    kspec = pl.BlockSpec((PEER_HEADS, PEER_KEYS, PEER_HALF), lambda i: (0, 0, 0))
    return pl.pallas_call(
        functools.partial(_route_body, tt=tt),
        grid=(n // tt,),
        in_specs=[pl.BlockSpec((tt, PEER_HEADS * PEER_QDIM), lambda i: (i, 0)), kspec, kspec],
        out_specs=[big_spec, big_spec, big_spec, big_spec],
        out_shape=[jax.ShapeDtypeStruct(shape, BF16), jax.ShapeDtypeStruct(shape, BF16),
                   jax.ShapeDtypeStruct(shape, F32), jax.ShapeDtypeStruct(shape, F32)],
        scratch_shapes=[pltpu.VMEM((PEER_HEADS, PEER_KEYS, tt), F32),
                        pltpu.VMEM((PEER_TOPK, PEER_HEADS, tt), F32),
                        pltpu.VMEM((PEER_TOPK, PEER_HEADS, tt), F32),
                        pltpu.VMEM((len(_PEER_PAIRS), PEER_HEADS, tt), F32)],
        compiler_params=_params("parallel"),
        name="peer_route",
    )(pq, k1, k2)


PEER_EB = 1024


def _coef_body(u_ref, x_ref, r2_ref, e2_ref, na_ref, e1_ref, o_ref, *, tt):
    nb = PEER_EB // PEER_KEYS
    e = pl.program_id(1)
    act = lax.dot_general(u_ref[...], x_ref[...], _NT, preferred_element_type=F32)
    gdt = r2_ref.dtype
    for j in range(nb):
        i1 = e * nb + j
        gate = jnp.zeros((PEER_KEYS, tt), gdt)
        for h in range(PEER_HEADS):
            na = jnp.broadcast_to(na_ref[h, pl.ds(i1, 1), :].astype(gdt), (PEER_KEYS, tt))
            e1 = jnp.broadcast_to(e1_ref[h, pl.ds(i1, 1), :].astype(gdt), (PEER_KEYS, tt))
            gate = gate + jnp.where(r2_ref[h] < na, e1 * e2_ref[h], jnp.zeros((), gdt))
        a = act[j * PEER_KEYS:(j + 1) * PEER_KEYS, :]
        gelu = 0.5 * a * (1.0 + lax.erf(a * math.sqrt(0.5)))
        o_ref[j * PEER_KEYS:(j + 1) * PEER_KEYS, :] = (gate * gelu.astype(gdt)).astype(o_ref.dtype)


def _peer_coef(u_tab, xn, r2, e2, na, e1):
    ne, d = u_tab.shape
    n = xn.shape[0]
    tt = _pick(n, (512, 256, 128))
    big_spec = pl.BlockSpec((PEER_HEADS, PEER_KEYS, tt), lambda i, e: (0, 0, i))
    return pl.pallas_call(
        functools.partial(_coef_body, tt=tt),
        grid=(n // tt, ne // PEER_EB),
        in_specs=[pl.BlockSpec((PEER_EB, d), lambda i, e: (e, 0)),
                  pl.BlockSpec((tt, d), lambda i, e: (i, 0)),
                  big_spec, big_spec, big_spec, big_spec],
        out_specs=pl.BlockSpec((PEER_EB, tt), lambda i, e: (e, i)),
        out_shape=jax.ShapeDtypeStruct((ne, n), BF16),
        compiler_params=_params("parallel", "arbitrary"),
        name="peer_coef",
    )(u_tab, xn, r2, e2, na, e1)


def _round_up(x, m):
    return -(-x // m) * m


def _layer(x, pos0, conv_prev, k_past, v_past, ki_past, mem_k, mem_v, w):
    b, t, d = x.shape
    n = b * t
    ch = d // 2
    n_heads = (d - ch) // HEAD_DIM
    off = w["offsets"]
    pos = pos0 + jnp.arange(t, dtype=jnp.int32)
    rope = _rope_tables(pos)

    x2 = x.reshape(n, d)
    hn = _rmsnorm(x2, w["norm_mix_g"])
    p = _matmul(hn, w["w_in"])
    u = _glu(p, off["a"], off["g"], ch)
    (qs,) = _head_post(p, off["q"], n_heads, t, (BF16,), gain=w["q_norm_g"], rope=rope, scale=ATTN_SCALE)
    k32, k16 = _head_post(p, off["k"], N_KV_HEADS, t, (F32, BF16), gain=w["k_norm_g"], rope=rope)
    (qi,) = _head_post(p, off["qi"], IDX_HEADS, t, (BF16,), rope=rope)
    ki32, ki16 = _head_post(p, off["ki"], 1, t, (F32, BF16), rope=rope)
    kvw = N_KV_HEADS * HEAD_DIM
    v32 = p[:, off["v"]:off["v"] + kvw]
    v16 = v32.astype(BF16)

    u3 = u.reshape(b, t, ch)
    prev = jnp.zeros((b, CONV_STATE, ch), F32) if conv_prev is None else conv_prev
    conv_out = _conv_module(u3, prev, w["dw_w"], w["dw_b"], w["conv_ln_g"], w["conv_ln_b"])
    conv_new = u3[:, t - CONV_STATE:, :]

    k3, v3, ki3 = k16.reshape(b, t, kvw), v16.reshape(b, t, kvw), ki16.reshape(b, t, IDX_DIM)
    if k_past is not None:
        past = k_past.shape[1]
        k3 = jnp.concatenate([k_past.reshape(b, past, kvw).astype(BF16), k3], axis=1)
        v3 = jnp.concatenate([v_past.reshape(b, past, kvw).astype(BF16), v3], axis=1)
        ki3 = jnp.concatenate([ki_past.astype(BF16), ki3], axis=1)
    s_valid = k3.shape[1]
    s_pad = _round_up(s_valid, DSA_SC)
    if s_pad != s_valid:
        padw = ((0, 0), (0, s_pad - s_valid), (0, 0))
        k3, v3, ki3 = jnp.pad(k3, padw), jnp.pad(v3, padw), jnp.pad(ki3, padw)
    topk = min(TOPK_MAX, s_valid // 4)
    attn = _dsa_attention(qs.reshape(b, t, n_heads * HEAD_DIM), k3, v3, qi.reshape(b, t, IDX_HEADS * IDX_DIM),
                          ki3, p.reshape(b, t, p.shape[1]), off["wi"], s_valid, topk, pos0)

    mix = jnp.concatenate([conv_out.reshape(n, ch), attn.reshape(n, n_heads * HEAD_DIM)], axis=1)
    h1 = _matmul(mix, w["w_out"], res=x2)

    hn2 = _rmsnorm(h1, w["norm_mem_g"])
    qm = _matmul(hn2, w["w_q_mem"])
    om = _mem_attention(qm.reshape(b, t, MEM_DIM), w["mem_q_norm_g"], mem_k, mem_v)
    h2 = _matmul(om.reshape(n, MEM_DIM), w["w_o_mem"], res=h1)

    hn3 = _rmsnorm(h2, w["norm_ffn_g"])
    pq = _matmul(hn3, w["peer_wq"], out_dtype=BF16)
    r2, e2, na, e1 = _peer_route(pq, w["peer_sub_k1"], w["peer_sub_k2"])
    coef_t = _peer_coef(w["peer_u"], hn3, r2, e2, na, e1)
    ne = coef_t.shape[0]
    y = _matmul(coef_t, w["peer_v"], res=h2, trans_a=True,
                tm=_pick(n, (1024, 512)), tn=_pick(d, (1024, 512)), tk=_pick(ne, (2048, 1024)))

    return (y.reshape(b, t, d), k32.reshape(b, t, N_KV_HEADS, HEAD_DIM), v32.reshape(b, t, N_KV_HEADS, HEAD_DIM),
            ki32.reshape(b, t, IDX_DIM), conv_new)


def _mem_kv(mem, mem_norm_g, w_kv, mem_k_norm_g):
    b, m, d = mem.shape
    mn = _rmsnorm(mem.reshape(b * m, d), mem_norm_g)
    mkv = _matmul(mn, w_kv)
    mk32, mk16 = _head_post(mkv, 0, MEM_HEADS, m, (F32, BF16), gain=mem_k_norm_g)
    mv32 = mkv[:, MEM_DIM:]
    return mk32.reshape(b, m, MEM_DIM), mv32.reshape(b, m, MEM_DIM), mk16.reshape(b, m, MEM_DIM)


def _prep_layer_weights(l, d, norm_mix_g, w_in, dw_w, dw_b, conv_ln_g, conv_ln_b, q_norm_g, k_norm_g, w_out,
                        norm_mem_g, w_q_mem, mem_q_norm_g, w_o_mem, norm_ffn_g, peer_wq, peer_sub_k1,
                        peer_sub_k2, peer_u, peer_v):
    ch = d // 2
    nq = (d - ch)
    kvw = N_KV_HEADS * HEAD_DIM
    niq = IDX_HEADS * IDX_DIM
    src = {"a": 0, "g": ch, "q": 2 * ch}
    src["k"] = src["q"] + nq
    src["v"] = src["k"] + kvw
    src["qi"] = src["v"] + kvw
    src["ki"] = src["qi"] + niq
    src["wi"] = src["ki"] + IDX_DIM
    widths = {"qi": niq, "a": ch, "g": ch, "q": nq, "k": kvw, "v": kvw, "ki": IDX_DIM, "wi": IDX_HEADS}
    order = ["qi", "a", "g", "q", "k", "v", "ki", "wi"]
    wl = w_in[l]
    cols, offsets, o = [], {}, 0
    for name in order:
        offsets[name] = o
        cols.append(wl[:, src[name]:src[name] + widths[name]].astype(BF16))
        o += widths[name]
    pad = _round_up(o, LANES) - o
    cols.append(jnp.zeros((d, pad), BF16))
    return {
        "offsets": offsets,
        "w_in": jnp.concatenate(cols, axis=1),
        "norm_mix_g": norm_mix_g[l], "dw_w": dw_w[l], "dw_b": dw_b[l],
        "conv_ln_g": conv_ln_g[l], "conv_ln_b": conv_ln_b[l],
        "q_norm_g": q_norm_g[l], "k_norm_g": k_norm_g[l],
        "w_out": w_out[l].astype(BF16),
        "norm_mem_g": norm_mem_g[l], "w_q_mem": w_q_mem[l].astype(BF16), "mem_q_norm_g": mem_q_norm_g[l],
        "w_o_mem": w_o_mem[l].astype(BF16),
        "norm_ffn_g": norm_ffn_g[l], "peer_wq": peer_wq[l].astype(BF16),
        "peer_sub_k1": peer_sub_k1[l].astype(BF16), "peer_sub_k2": peer_sub_k2[l].astype(BF16),
        "peer_u": peer_u[l].astype(BF16), "peer_v": peer_v[l].astype(BF16),
    }


def kernel(x_prompt, x_sample, mem_prompt, cache_k, cache_v, cache_k_idx, state_conv, cache_mem_k, cache_mem_v, norm_mix_g, w_in, dw_w, dw_b, conv_ln_g, conv_ln_b, q_norm_g, k_norm_g, w_out, norm_mem_g, mem_norm_g, w_q_mem, w_k_mem, w_v_mem, mem_q_norm_g, mem_k_norm_g, w_o_mem, norm_ffn_g, peer_wq, peer_sub_k1, peer_sub_k2, peer_u, peer_v):
    depth = w_in.shape[0]
    d = x_prompt.shape[-1]
    past_len = cache_k.shape[2]
    h_p, h_s = x_prompt, x_sample
    outs = [[] for _ in range(10)]
    for l in range(depth):
        w = _prep_layer_weights(l, d, norm_mix_g, w_in, dw_w, dw_b, conv_ln_g, conv_ln_b, q_norm_g, k_norm_g,
                                w_out, norm_mem_g, w_q_mem, mem_q_norm_g, w_o_mem, norm_ffn_g, peer_wq,
                                peer_sub_k1, peer_sub_k2, peer_u, peer_v)
        w_kv = jnp.concatenate([w_k_mem[l], w_v_mem[l]], axis=1).astype(BF16)
        mk32, mv32, mk16 = _mem_kv(mem_prompt, mem_norm_g[l], w_kv, mem_k_norm_g[l])
        h_p, kp, vp, kip, cp = _layer(h_p, 0, None, None, None, None, mk16, mv32.astype(BF16), w)
        bs, ms = cache_mem_k.shape[1], cache_mem_k.shape[2]
        h_s, ks, vs, kis, cs = _layer(h_s, past_len, state_conv[l], cache_k[l], cache_v[l], cache_k_idx[l],
                                      cache_mem_k[l].reshape(bs, ms, MEM_DIM).astype(BF16),
                                      cache_mem_v[l].reshape(bs, ms, MEM_DIM).astype(BF16), w)
        bp, mp = mem_prompt.shape[0], mem_prompt.shape[1]
        for lst, val in zip(outs, (kp, vp, kip, cp, mk32.reshape(bp, mp, MEM_HEADS, HEAD_DIM),
                                   mv32.reshape(bp, mp, MEM_HEADS, HEAD_DIM), ks, vs, kis, cs)):
            lst.append(val)
    return (h_p, h_s) + tuple(jnp.stack(lst) for lst in outs)
```

```python
import functools
import math

import jax
import jax.numpy as jnp
from jax import lax
from jax.experimental import pallas as pl
from jax.experimental.pallas import tpu as pltpu

F32 = jnp.float32
BF16 = jnp.bfloat16

CHUNK = 64
CONV_WIDTH = 31
CONV_STATE = CONV_WIDTH - 1
HEAD_DIM = 128
N_KV_HEADS = 4
ROPE_DIM = HEAD_DIM // 4
ROPE_THETA = 500000.0
IDX_HEADS = 32
IDX_DIM = 128
IDX_SCALE = (IDX_HEADS ** -0.5) * (IDX_DIM ** -0.5)
TOPK_MAX = 256
ATTN_SCALE = HEAD_DIM ** -0.5
MEM_HEADS = 4
MEM_DIM = MEM_HEADS * HEAD_DIM
PEER_KEYS = 128
PEER_HEADS = 8
PEER_QDIM = 256
PEER_TOPK = 16
EPS = 1e-6

LANES = 128
VMEM_LIMIT_BYTES = 56 * 1024 * 1024

NEG_BIG = -1e30
INT_MIN = -(2 ** 31)

_NT = (((1,), (1,)), ((), ()))
_NN = (((1,), (0,)), ((), ()))
_TN = (((0,), (0,)), ((), ()))


def _params(*sem):
    return pltpu.CompilerParams(dimension_semantics=sem, vmem_limit_bytes=VMEM_LIMIT_BYTES)


def _pick(n, candidates):
    for c in candidates:
        if c <= n and n % c == 0:
            return c
    return n


def _rmsnorm_body(x_ref, g_ref, o_ref):
    x = x_ref[...]
    ms = jnp.mean(x * x, axis=-1, keepdims=True)
    o_ref[...] = (x * lax.rsqrt(ms + EPS) * g_ref[...]).astype(o_ref.dtype)


def _rmsnorm(x, g):
    n, d = x.shape
    tr = _pick(n, (256, 128, 64))
    return pl.pallas_call(
        _rmsnorm_body,
        grid=(n // tr,),
        in_specs=[pl.BlockSpec((tr, d), lambda i: (i, 0)),
                  pl.BlockSpec((1, d), lambda i: (0, 0))],
        out_specs=pl.BlockSpec((tr, d), lambda i: (i, 0)),
        out_shape=jax.ShapeDtypeStruct((n, d), BF16),
        compiler_params=_params("parallel"),
        name="rmsnorm",
    )(x, g.reshape(1, d))


def _mm_body(*refs, nk, dims, has_res):
    a_ref, b_ref = refs[0], refs[1]
    r_ref = refs[2] if has_res else None
    o_ref = refs[2 + has_res]
    p = lax.dot_general(a_ref[...], b_ref[...], dims, preferred_element_type=F32)
    if nk == 1:
        if has_res:
            p = r_ref[...] + p
        o_ref[...] = p.astype(o_ref.dtype)
        return
    acc_ref = refs[3 + has_res]
    k = pl.program_id(2)

    @pl.when(k == 0)
    def _():
        acc_ref[...] = p

    @pl.when(k > 0)
    def _():
        acc_ref[...] += p

    @pl.when(k == nk - 1)
    def _():
        r = acc_ref[...]
        if has_res:
            r = r_ref[...] + r
        o_ref[...] = r.astype(o_ref.dtype)


def _matmul(a, b, res=None, out_dtype=F32, trans_a=False, tm=None, tn=None, tk=None):
    if trans_a:
        kdim, m = a.shape
    else:
        m, kdim = a.shape
    n = b.shape[1]
    tm = tm or _pick(m, (1024, 512, 256, 128, 64))
    tn = tn or _pick(n, (768, 512, 384, 256, 128))
    tk = tk or kdim
    nk = kdim // tk
    if trans_a:
        a_spec = pl.BlockSpec((tk, tm), lambda i, j, k: (k, i))
        dims = _TN
    else:
        a_spec = pl.BlockSpec((tm, tk), lambda i, j, k: (i, k))
        dims = _NN
    in_specs = [a_spec, pl.BlockSpec((tk, tn), lambda i, j, k: (k, j))]
    args = [a, b]
    if res is not None:
        in_specs.append(pl.BlockSpec((tm, tn), lambda i, j, k: (i, j)))
        args.append(res)
    scratch = [pltpu.VMEM((tm, tn), F32)] if nk > 1 else []
    return pl.pallas_call(
        functools.partial(_mm_body, nk=nk, dims=dims, has_res=res is not None),
        grid=(m // tm, n // tn, nk),
        in_specs=in_specs,
        out_specs=pl.BlockSpec((tm, tn), lambda i, j, k: (i, j)),
        out_shape=jax.ShapeDtypeStruct((m, n), out_dtype),
        scratch_shapes=scratch,
        compiler_params=_params("parallel", "parallel", "arbitrary"),
        name="matmul",
    )(*args)


def _mm2_body(a1_ref, a2_ref, b_ref, r_ref, o_ref):
    k1 = a1_ref.shape[1]
    p = lax.dot_general(a1_ref[...], b_ref[0:k1, :], _NN, preferred_element_type=F32)
    p = p + lax.dot_general(a2_ref[...], b_ref[k1:, :], _NN, preferred_element_type=F32)
    o_ref[...] = (r_ref[...] + p).astype(o_ref.dtype)


def _matmul_cat(a1, a2, b, res):
    m, k1 = a1.shape
    k2 = a2.shape[1]
    n = b.shape[1]
    tm = _pick(m, (1024, 512, 256, 128, 64))
    tn = _pick(n, (768, 512, 384, 256, 128))
    return pl.pallas_call(
        _mm2_body,
        grid=(m // tm, n // tn),
        in_specs=[pl.BlockSpec((tm, k1), lambda i, j: (i, 0)),
                  pl.BlockSpec((tm, k2), lambda i, j: (i, 0)),
                  pl.BlockSpec((k1 + k2, tn), lambda i, j: (0, j)),
                  pl.BlockSpec((tm, tn), lambda i, j: (i, j))],
        out_specs=pl.BlockSpec((tm, tn), lambda i, j: (i, j)),
        out_shape=jax.ShapeDtypeStruct((m, n), F32),
        compiler_params=_params("parallel", "parallel"),
        name="matmul_cat",
    )(a1, a2, b, res)


def _head_body(*refs, nh, has_norm, has_rope, scale, n_out):
    it = iter(refs)
    p_ref = next(it)
    g_ref = next(it) if has_norm else None
    if has_rope:
        c = next(it)[...]
        sa = next(it)[...]
        sb = next(it)[...]
    outs = [next(it) for _ in range(n_out)]
    for h in range(nh):
        sl = slice(h * HEAD_DIM, (h + 1) * HEAD_DIM)
        x = p_ref[:, sl]
        if has_norm:
            ms = jnp.mean(x * x, axis=-1, keepdims=True)
            x = x * lax.rsqrt(ms + EPS) * g_ref[...]
        if has_rope:
            x = x * c + pltpu.roll(x, LANES - ROPE_DIM // 2, 1) * sa + pltpu.roll(x, ROPE_DIM // 2, 1) * sb
        if scale != 1.0:
            x = x * scale
        for o in outs:
            o[:, sl] = x.astype(o.dtype)


def _head_post(p, col0, nh, t_len, out_dtypes, gain=None, rope=None, scale=1.0):
    n = p.shape[0]
    w = nh * HEAD_DIM
    assert col0 % w == 0
    tr = _pick(t_len, (256, 128, 64))
    tpb = t_len // tr
    in_specs = [pl.BlockSpec((tr, w), lambda i: (i, col0 // w))]
    args = [p]
    if gain is not None:
        in_specs.append(pl.BlockSpec((1, HEAD_DIM), lambda i: (0, 0)))
        args.append(gain.reshape(1, HEAD_DIM))
    if rope is not None:
        for tab in rope:
            in_specs.append(pl.BlockSpec((tr, LANES), lambda i: (i % tpb, 0)))
            args.append(tab)
    outs = pl.pallas_call(
        functools.partial(_head_body, nh=nh, has_norm=gain is not None, has_rope=rope is not None,
                          scale=scale, n_out=len(out_dtypes)),
        grid=(n // tr,),
        in_specs=in_specs,
        out_specs=[pl.BlockSpec((tr, w), lambda i: (i, 0)) for _ in out_dtypes],
        out_shape=[jax.ShapeDtypeStruct((n, w), dt) for dt in out_dtypes],
        compiler_params=_params("parallel"),
        name="head_post",
    )(*args)
    return outs


def _rope_tables(pos):
    half = ROPE_DIM // 2
    inv_freq = jnp.power(ROPE_THETA, -jnp.arange(half, dtype=F32) / half)
    ang = pos.astype(F32)[:, None] * inv_freq[None, :]
    cos, sin = jnp.cos(ang), jnp.sin(ang)
    t = pos.shape[0]
    zh = jnp.zeros((t, half), F32)
    zr = jnp.zeros((t, HEAD_DIM - ROPE_DIM), F32)
    c = jnp.concatenate([cos, cos, jnp.ones((t, HEAD_DIM - ROPE_DIM), F32)], axis=1)
    sa = jnp.concatenate([-sin, zh, zr], axis=1)
    sb = jnp.concatenate([zh, sin, zr], axis=1)
    return c, sa, sb


def _norm_heads(x, gain):
    ms = jnp.mean(x * x, axis=-1, keepdims=True)
    return x * lax.rsqrt(ms + EPS) * gain


def _rope_heads(x, c, sa, sb):
    return x * c + pltpu.roll(x, LANES - ROPE_DIM // 2, 1) * sa + pltpu.roll(x, ROPE_DIM // 2, 1) * sb


def _proj_heads_body(a_ref, b_ref, g_ref, c_ref, sa_ref, sb_ref, o_ref, *, has_norm, scale):
    p = lax.dot_general(a_ref[...], b_ref[...], _NN, preferred_element_type=F32)
    c, sa, sb = c_ref[...], sa_ref[...], sb_ref[...]
    for h in range(p.shape[1] // HEAD_DIM):
        sl = slice(h * HEAD_DIM, (h + 1) * HEAD_DIM)
        x = p[:, sl]
        if has_norm:
            x = _norm_heads(x, g_ref[...])
        x = _rope_heads(x, c, sa, sb)
        if scale != 1.0:
            x = x * scale
        o_ref[:, sl] = x.astype(o_ref.dtype)


def _proj_heads(a, b, col0, width, rope, tm, gain=None, scale=1.0):
    m, kdim = a.shape
    tn = _pick(width, (1024, 512, 256, 128))
    assert col0 % tn == 0
    tpb = rope[0].shape[0] // tm
    g = jnp.ones((HEAD_DIM,), F32) if gain is None else gain
    tab_spec = pl.BlockSpec((tm, LANES), lambda i, j: (i % tpb, 0))
    return pl.pallas_call(
        functools.partial(_proj_heads_body, has_norm=gain is not None, scale=scale),
        grid=(m // tm, width // tn),
        in_specs=[pl.BlockSpec((tm, kdim), lambda i, j: (i, 0)),
                  pl.BlockSpec((kdim, tn), lambda i, j: (0, col0 // tn + j)),
                  pl.BlockSpec((1, HEAD_DIM), lambda i, j: (0, 0)),
                  tab_spec, tab_spec, tab_spec],
        out_specs=pl.BlockSpec((tm, tn), lambda i, j: (i, j)),
        out_shape=jax.ShapeDtypeStruct((m, width), BF16),
        compiler_params=_params("parallel", "parallel"),
        name="proj_heads",
    )(a, b, g.reshape(1, HEAD_DIM), *rope)


def _proj_glu_body(a_ref, ba_ref, bg_ref, o_ref):
    x = a_ref[...]
    pa = lax.dot_general(x, ba_ref[...], _NN, preferred_element_type=F32)
    pg = lax.dot_general(x, bg_ref[...], _NN, preferred_element_type=F32)
    o_ref[...] = pa * jax.nn.sigmoid(pg)


def _proj_glu(a, b, col_a, col_g, width, tm):
    m, kdim = a.shape
    tn = _pick(width, (512, 256, 128))
    assert col_a % tn == 0 and col_g % tn == 0
    return pl.pallas_call(
        _proj_glu_body,
        grid=(m // tm, width // tn),
        in_specs=[pl.BlockSpec((tm, kdim), lambda i, j: (i, 0)),
                  pl.BlockSpec((kdim, tn), lambda i, j: (0, col_a // tn + j)),
                  pl.BlockSpec((kdim, tn), lambda i, j: (0, col_g // tn + j))],
        out_specs=pl.BlockSpec((tm, tn), lambda i, j: (i, j)),
        out_shape=jax.ShapeDtypeStruct((m, width), F32),
        compiler_params=_params("parallel", "parallel"),
        name="proj_glu",
    )(a, b, b)


KV_W = N_KV_HEADS * HEAD_DIM
TAIL_W = 2 * KV_W + IDX_DIM + LANES


def _proj_tail_body(a_ref, b_ref, g_ref, c_ref, sa_ref, sb_ref, k32_ref, k16_ref, v32_ref, v16_ref,
                    ki32_ref, ki16_ref, wi_ref):
    p = lax.dot_general(a_ref[...], b_ref[...], _NN, preferred_element_type=F32)
    c, sa, sb = c_ref[...], sa_ref[...], sb_ref[...]
    for h in range(N_KV_HEADS):
        sl = slice(h * HEAD_DIM, (h + 1) * HEAD_DIM)
        x = _rope_heads(_norm_heads(p[:, sl], g_ref[...]), c, sa, sb)
        k32_ref[:, sl] = x
        k16_ref[:, sl] = x.astype(k16_ref.dtype)
    v = p[:, KV_W:2 * KV_W]
    v32_ref[...] = v
    v16_ref[...] = v.astype(v16_ref.dtype)
    ki = _rope_heads(p[:, 2 * KV_W:2 * KV_W + IDX_DIM], c, sa, sb)
    ki32_ref[...] = ki
    ki16_ref[...] = ki.astype(ki16_ref.dtype)
    wi_ref[...] = p[:, 2 * KV_W + IDX_DIM:]


def _proj_tail(a, b, col0, rope, tm, k_gain):
    m, kdim = a.shape
    assert col0 % TAIL_W == 0
    tpb = rope[0].shape[0] // tm
    tab_spec = pl.BlockSpec((tm, LANES), lambda i: (i % tpb, 0))
    widths = (KV_W, KV_W, KV_W, KV_W, IDX_DIM, IDX_DIM, LANES)
    dtypes = (F32, BF16, F32, BF16, F32, BF16, F32)
    return pl.pallas_call(
        _proj_tail_body,
        grid=(m // tm,),
        in_specs=[pl.BlockSpec((tm, kdim), lambda i: (i, 0)),
                  pl.BlockSpec((kdim, TAIL_W), lambda i: (0, col0 // TAIL_W)),
                  pl.BlockSpec((1, HEAD_DIM), lambda i: (0, 0)),
                  tab_spec, tab_spec, tab_spec],
        out_specs=[pl.BlockSpec((tm, wd), lambda i: (i, 0)) for wd in widths],
        out_shape=[jax.ShapeDtypeStruct((m, wd), dt) for wd, dt in zip(widths, dtypes)],
        compiler_params=_params("parallel"),
        name="proj_tail",
    )(a, b, k_gain.reshape(1, HEAD_DIM), *rope)


CONV_HALO = 32
CONV_CW = 256
CONV_TB = 64
SUBLANES = 8


def _conv_body(u_ref, prev_ref, w_ref, b_ref, g_ref, beta_ref, o_ref, win_ref, rot_ref, c_ref, *, tt, ch):
    @pl.when(pl.program_id(1) == 0)
    def _():
        win_ref[0:CONV_HALO, :] = prev_ref[0]

    win_ref[CONV_HALO:CONV_HALO + tt, :] = u_ref[0]
    base = CONV_HALO - CONV_STATE
    for cc in range(ch // CONV_CW):
        cs = slice(cc * CONV_CW, (cc + 1) * CONV_CW)
        for r in range(SUBLANES):
            rows = SUBLANES * ((CONV_WIDTH - 1 - r) // SUBLANES) + tt
            rot_ref[r, 0:rows, :] = win_ref[base + r:base + r + rows, cs]
        for tb in range(tt // CONV_TB):
            acc = None
            for j in range(CONV_WIDTH):
                q, r = divmod(j, SUBLANES)
                lo = SUBLANES * q + tb * CONV_TB
                term = rot_ref[r, lo:lo + CONV_TB, :] * w_ref[j:j + 1, cs]
                acc = term if acc is None else acc + term
            c_ref[tb * CONV_TB:(tb + 1) * CONV_TB, cs] = acc + b_ref[:, cs]
    c = c_ref[...]
    mu = jnp.mean(c, axis=-1, keepdims=True)
    d = c - mu
    var = jnp.mean(d * d, axis=-1, keepdims=True)
    y = d * lax.rsqrt(var + EPS) * g_ref[...] + beta_ref[...]
    o_ref[0] = (y * jax.nn.sigmoid(y)).astype(o_ref.dtype)
    win_ref[0:CONV_HALO, :] = win_ref[tt:tt + CONV_HALO, :]


def _conv_module(u, prev, dw_w, dw_b, ln_g, ln_b):
    b, t, ch = u.shape
    tt = _pick(t, (256, 128, 64))
    prev_p = jnp.pad(prev, ((0, 0), (CONV_HALO - CONV_STATE, 0), (0, 0)))
    w_p = jnp.pad(dw_w, ((0, CONV_HALO - CONV_WIDTH), (0, 0)))
    row = lambda v: v.reshape(1, ch)
    rot_rows = SUBLANES * ((CONV_WIDTH - 1) // SUBLANES) + tt
    return pl.pallas_call(
        functools.partial(_conv_body, tt=tt, ch=ch),
        grid=(b, t // tt),
        in_specs=[pl.BlockSpec((1, tt, ch), lambda i, j: (i, j, 0)),
                  pl.BlockSpec((1, CONV_HALO, ch), lambda i, j: (i, 0, 0)),
                  pl.BlockSpec((CONV_HALO, ch), lambda i, j: (0, 0)),
                  pl.BlockSpec((1, ch), lambda i, j: (0, 0)),
                  pl.BlockSpec((1, ch), lambda i, j: (0, 0)),
                  pl.BlockSpec((1, ch), lambda i, j: (0, 0))],
        out_specs=pl.BlockSpec((1, tt, ch), lambda i, j: (i, j, 0)),
        out_shape=jax.ShapeDtypeStruct((b, t, ch), BF16),
        scratch_shapes=[pltpu.VMEM((tt + CONV_HALO, ch), F32),
                        pltpu.VMEM((SUBLANES, rot_rows, CONV_CW), F32),
                        pltpu.VMEM((tt, ch), F32)],
        compiler_params=_params("arbitrary", "arbitrary"),
        name="conv_module",
    )(u, prev_p, w_p, row(dw_b), row(ln_g), row(ln_b))


DSA_SC = 512


def _dsa_body(qs_ref, k_ref, v_ref, qi_ref, ki_ref, wi_ref, o_ref, key_ref, bias_ref, thr_ref, qst_ref,
              mx_ref, acc_ref, *, tq, nchunks, buckets, s_valid, topk, pos0, n_heads):
    sc = DSA_SC
    group = n_heads // N_KV_HEADS
    rows = group * tq
    q_lo = pos0 + pl.program_id(1) * tq
    qpos = q_lo + lax.broadcasted_iota(jnp.int32, (tq, 1), 0)
    chunk_shift = CHUNK.bit_length() - 1
    qchunk = qpos >> chunk_shift
    k_end = jnp.minimum((((q_lo + tq - 1) >> chunk_shift) + 1) << chunk_shift, s_valid)
    nact = jnp.minimum((k_end + sc - 1) >> (sc.bit_length() - 1), nchunks)
    nbucket = jnp.int32(buckets[-1])
    for bsz in reversed(buckets[:-1]):
        nbucket = jnp.where(nact <= bsz, bsz, nbucket)

    def score_chunk(c, carry):
        off = pl.multiple_of(c * sc, sc)
        ki_c = ki_ref[0, pl.ds(off, sc), :]
        acc = jnp.zeros((tq, sc), F32)
        for h in range(IDX_HEADS):
            s = lax.dot_general(qi_ref[0, :, h * IDX_DIM:(h + 1) * IDX_DIM], ki_c, _NT,
                                preferred_element_type=F32)
            acc = acc + wi_ref[0, :, h:h + 1] * jnp.maximum(s, 0.0)
        score = acc * IDX_SCALE
        spos = off + lax.broadcasted_iota(jnp.int32, (1, sc), 1)
        schunk = spos >> chunk_shift
        adm = jnp.logical_and(schunk <= qchunk, spos < s_valid)
        bits = pltpu.bitcast(score, jnp.int32)
        key = bits ^ ((bits >> 31) & 0x7FFFFFFF)
        key_ref[c] = jnp.where(adm, key, INT_MIN)
        return carry

    lax.fori_loop(0, nact, score_chunk, 0)

    def fill_chunk(c, carry):
        key_ref[c] = jnp.full((tq, sc), INT_MIN, jnp.int32)
        return carry

    lax.fori_loop(nact, nbucket, fill_chunk, 0)

    for bsz in buckets:
        @pl.when(nbucket == bsz)
        def _(bsz=bsz):
            def count_ge(cand):
                ge = (key_ref[0:bsz] >= cand[None]).astype(F32)
                return jnp.sum(jnp.sum(ge, axis=0), axis=-1, keepdims=True)

            zero = jnp.zeros((tq, 1), jnp.int32)
            ans = jnp.where(count_ge(zero) >= topk, zero, INT_MIN)

            def bit_step(i, ans):
                cand = ans | lax.shift_left(jnp.int32(1), 30 - i)
                return jnp.where(count_ge(cand) >= topk, cand, ans)

            ans = lax.fori_loop(0, 31, bit_step, ans)
            thr_ref[...] = jnp.maximum(ans, INT_MIN + 1)

    thr = thr_ref[...]

    def bias_chunk(c, carry):
        bias_ref[c] = jnp.where(key_ref[c] >= thr, 0.0, NEG_BIG)
        return carry

    lax.fori_loop(0, nact, bias_chunk, 0)

    for hd in range(n_heads):
        qst_ref[hd * tq:(hd + 1) * tq, :] = qs_ref[0, :, hd * HEAD_DIM:(hd + 1) * HEAD_DIM]
    ntile = sc // LANES

    def masked_logits(c, g):
        off = pl.multiple_of(c * sc, sc)
        k_c = k_ref[0, pl.ds(off, sc), g * HEAD_DIM:(g + 1) * HEAD_DIM]
        lg = lax.dot_general(qst_ref[g * rows:(g + 1) * rows, :], k_c, _NT, preferred_element_type=F32)
        return (lg.reshape(group, tq, sc) + bias_ref[c][None]).reshape(rows, sc)

    mx_ref[...] = jnp.full(mx_ref.shape, NEG_BIG, F32)

    def max_chunk(c, carry):
        for g in range(N_KV_HEADS):
            rs = slice(g * rows, (g + 1) * rows)
            lg = masked_logits(c, g)
            mx = mx_ref[rs, :]
            for t in range(ntile):
                mx = jnp.maximum(mx, lg[:, t * LANES:(t + 1) * LANES])
            mx_ref[rs, :] = mx
        return carry

    lax.fori_loop(0, nact, max_chunk, 0)
    mx_ref[...] = jnp.broadcast_to(jnp.max(mx_ref[...], axis=-1, keepdims=True), mx_ref.shape)

    acc_ref[...] = jnp.zeros(acc_ref.shape, F32)
    ones = jnp.ones((sc, HEAD_DIM), BF16)

    def attn_chunk(c, carry):
        off = pl.multiple_of(c * sc, sc)
        for g in range(N_KV_HEADS):
            rs = slice(g * rows, (g + 1) * rows)
            lg = masked_logits(c, g)
            mb = mx_ref[rs, :]
            p = jnp.concatenate([jnp.exp(lg[:, t * LANES:(t + 1) * LANES] - mb) for t in range(ntile)], axis=1)
            v_ext = jnp.concatenate([v_ref[0, pl.ds(off, sc), g * HEAD_DIM:(g + 1) * HEAD_DIM], ones], axis=1)
            acc_ref[rs, :] += lax.dot_general(p.astype(BF16), v_ext, _NN, preferred_element_type=F32)
        return carry

    lax.fori_loop(0, nact, attn_chunk, 0)
    for hd in range(n_heads):
        hs = slice(hd * tq, (hd + 1) * tq)
        o = acc_ref[hs, 0:HEAD_DIM] / acc_ref[hs, HEAD_DIM:2 * HEAD_DIM]
        o_ref[0, :, hd * HEAD_DIM:(hd + 1) * HEAD_DIM] = o.astype(o_ref.dtype)


def _dsa_attention(qs, k_all, v_all, qi, ki_all, p3, wi_col, s_valid, topk, pos0):
    b, t, qw = qs.shape
    s_pad = k_all.shape[1]
    n_heads = qw // HEAD_DIM
    tq = _pick(t, (128, 64))
    nchunks = s_pad // DSA_SC
    if pos0 + tq >= s_valid:
        buckets = (nchunks,)
    else:
        buckets = tuple(sorted({min(1 << e, nchunks) for e in range(nchunks.bit_length() + 1)}))
    return pl.pallas_call(
        functools.partial(_dsa_body, tq=tq, nchunks=nchunks, buckets=buckets, s_valid=s_valid, topk=topk,
                          pos0=pos0, n_heads=n_heads),
        grid=(b, t // tq),
        in_specs=[pl.BlockSpec((1, tq, qw), lambda i, j: (i, j, 0)),
                  pl.BlockSpec((1, s_pad, N_KV_HEADS * HEAD_DIM), lambda i, j: (i, 0, 0)),
                  pl.BlockSpec((1, s_pad, N_KV_HEADS * HEAD_DIM), lambda i, j: (i, 0, 0)),
                  pl.BlockSpec((1, tq, IDX_HEADS * IDX_DIM), lambda i, j: (i, j, 0)),
                  pl.BlockSpec((1, s_pad, IDX_DIM), lambda i, j: (i, 0, 0)),
                  pl.BlockSpec((1, tq, LANES), lambda i, j: (i, j, wi_col // LANES))],
        out_specs=pl.BlockSpec((1, tq, qw), lambda i, j: (i, j, 0)),
        out_shape=jax.ShapeDtypeStruct((b, t, qw), BF16),
        scratch_shapes=[pltpu.VMEM((nchunks, tq, DSA_SC), jnp.int32),
                        pltpu.VMEM((nchunks, tq, DSA_SC), F32),
                        pltpu.VMEM((tq, 1), jnp.int32),
                        pltpu.VMEM((n_heads * tq, HEAD_DIM), BF16),
                        pltpu.VMEM((n_heads * tq, LANES), F32),
                        pltpu.VMEM((n_heads * tq, 2 * HEAD_DIM), F32)],
        compiler_params=_params("parallel", "arbitrary"),
        name="dsa_attention",
    )(qs, k_all, v_all, qi, ki_all, p3)


def _memattn_body(q_ref, g_ref, mk_ref, mv_ref, o_ref):
    for h in range(MEM_HEADS):
        sl = slice(h * HEAD_DIM, (h + 1) * HEAD_DIM)
        x = q_ref[0, :, sl]
        ms = jnp.mean(x * x, axis=-1, keepdims=True)
        qn = (x * lax.rsqrt(ms + EPS) * g_ref[...] * ATTN_SCALE).astype(BF16)
        lg = lax.dot_general(qn, mk_ref[0, :, sl], _NT, preferred_element_type=F32)
        m = jnp.max(lg, axis=-1, keepdims=True)
        p = jnp.exp(lg - m)
        l = jnp.sum(p, axis=-1, keepdims=True)
        o = lax.dot_general(p.astype(BF16), mv_ref[0, :, sl], _NN, preferred_element_type=F32)
        o_ref[0, :, sl] = (o / l).astype(o_ref.dtype)


def _mem_attention(qm, gain, mk, mv):
    b, t, w = qm.shape
    m = mk.shape[1]
    tq = _pick(t, (512, 256, 128, 64))
    return pl.pallas_call(
        _memattn_body,
        grid=(b, t // tq),
        in_specs=[pl.BlockSpec((1, tq, w), lambda i, j: (i, j, 0)),
                  pl.BlockSpec((1, HEAD_DIM), lambda i, j: (0, 0)),
                  pl.BlockSpec((1, m, w), lambda i, j: (i, 0, 0)),
                  pl.BlockSpec((1, m, w), lambda i, j: (i, 0, 0))],
        out_specs=pl.BlockSpec((1, tq, w), lambda i, j: (i, j, 0)),
        out_shape=jax.ShapeDtypeStruct((b, t, w), BF16),
        compiler_params=_params("parallel", "parallel"),
        name="mem_attention",
    )(qm, gain.reshape(1, HEAD_DIM), mk, mv)


PEER_HALF = PEER_QDIM // 2
_PEER_PAIRS = [(a, b) for a in range(PEER_TOPK) for b in range(PEER_TOPK) if (a + 1) * (b + 1) <= PEER_TOPK]


def _route_body(pq_ref, k1_ref, k2_ref, r2_ref, e2_ref, na_ref, e1_ref, rk1_ref, v1_ref, v2_ref, c_ref, *, tt):
    row = lax.broadcasted_iota(jnp.int32, (PEER_KEYS, tt), 0)
    for h in range(PEER_HEADS):
        for table, (kref, vref, lo) in enumerate(((k1_ref, v1_ref, 0), (k2_ref, v2_ref, PEER_HALF))):
            q = pq_ref[:, h * PEER_QDIM + lo:h * PEER_QDIM + lo + PEER_HALF]
            s = lax.dot_general(kref[h], q, _NT, preferred_element_type=F32)
            cur = s
            rank = jnp.full((PEER_KEYS, tt), float(PEER_TOPK), F32)
            for r in range(PEER_TOPK):
                m = jnp.max(cur, axis=0, keepdims=True)
                vref[r, h:h + 1, :] = m
                first = jnp.min(jnp.where(cur == m, row, PEER_KEYS), axis=0, keepdims=True)
                hit = row == first
                cur = jnp.where(hit, -jnp.inf, cur)
                rank = jnp.where(hit, float(r), rank)
            ex = jnp.exp(s - vref[0, h:h + 1, :])
            if table == 0:
                rk1_ref[h] = rank
                e1_ref[h] = ex
            else:
                r2_ref[h] = rank.astype(r2_ref.dtype)
                e2_ref[h] = ex.astype(e2_ref.dtype)
    for p, (a, b) in enumerate(_PEER_PAIRS):
        c_ref[p] = v1_ref[a] + v2_ref[b]
    call = c_ref[...]

    def tau_step(p, tau):
        cp = c_ref[p]
        cnt = jnp.sum((call >= cp[None]).astype(F32), axis=0)
        return jnp.maximum(tau, jnp.where(cnt >= PEER_TOPK, cp, -jnp.inf))

    tau = lax.fori_loop(0, len(_PEER_PAIRS), tau_step, jnp.full((PEER_HEADS, tt), -jnp.inf, F32))
    zero = jnp.zeros((PEER_HEADS, tt), F32)
    cnt_gt = [zero] * PEER_TOPK
    cnt_eq = [zero] * PEER_TOPK
    for p, (a, b) in enumerate(_PEER_PAIRS):
        cnt_gt[a] = cnt_gt[a] + (call[p] > tau).astype(F32)
        cnt_eq[a] = cnt_eq[a] + (call[p] == tau).astype(F32)
    rem = float(PEER_TOPK) - sum(cnt_gt)
    n_of = []
    for a in range(PEER_TOPK):
        take = jnp.minimum(cnt_eq[a], rem)
        rem = rem - take
        n_of.append(cnt_gt[a] + take)
    z = zero
    for p, (a, b) in enumerate(_PEER_PAIRS):
        z = z + jnp.where(n_of[a] > float(b), jnp.exp(call[p] - call[0]), 0.0)
    inv_z = 1.0 / z
    for h in range(PEER_HEADS):
        rk = rk1_ref[h]
        na = jnp.zeros((PEER_KEYS, tt), F32)
        for a in range(PEER_TOPK):
            na = jnp.where(rk == float(a), n_of[a][h:h + 1, :], na)
        na_ref[h] = na
        e1_ref[h] = e1_ref[h] * inv_z[h:h + 1, :]


def _peer_route(pq, k1, k2):
    n = pq.shape[0]
    tt = _pick(n, (256, 128))
    shape = (PEER_HEADS, PEER_KEYS, n)
    big_spec = pl.BlockSpec((PEER_HEADS, PEER_KEYS, tt), lambda i: (0, 0, i))
    kspec = pl.BlockSpec((PEER_HEADS, PEER_KEYS, PEER_HALF), lambda i: (0, 0, 0))
    return pl.pallas_call(
        functools.partial(_route_body, tt=tt),
        grid=(n // tt,),
        in_specs=[pl.BlockSpec((tt, PEER_HEADS * PEER_QDIM), lambda i: (i, 0)), kspec, kspec],
        out_specs=[big_spec, big_spec, big_spec, big_spec],
        out_shape=[jax.ShapeDtypeStruct(shape, BF16), jax.ShapeDtypeStruct(shape, BF16),
                   jax.ShapeDtypeStruct(shape, F32), jax.ShapeDtypeStruct(shape, F32)],
        scratch_shapes=[pltpu.VMEM((PEER_HEADS, PEER_KEYS, tt), F32),
                        pltpu.VMEM((PEER_TOPK, PEER_HEADS, tt), F32),
                        pltpu.VMEM((PEER_TOPK, PEER_HEADS, tt), F32),
                        pltpu.VMEM((len(_PEER_PAIRS), PEER_HEADS, tt), F32)],
        compiler_params=_params("parallel"),
        name="peer_route",
    )(pq, k1, k2)


PEER_EB = 1024


def _coef_body(u_ref, x_ref, r2_ref, e2_ref, na_ref, e1_ref, o_ref, *, tt):
    nb = PEER_EB // PEER_KEYS
    e = pl.program_id(1)
    act = lax.dot_general(u_ref[...], x_ref[...], _NT, preferred_element_type=F32)
    gdt = r2_ref.dtype
    for j in range(nb):
        i1 = e * nb + j
        gate = jnp.zeros((PEER_KEYS, tt), gdt)
        for h in range(PEER_HEADS):
            na = jnp.broadcast_to(na_ref[h, pl.ds(i1, 1), :].astype(gdt), (PEER_KEYS, tt))
            e1 = jnp.broadcast_to(e1_ref[h, pl.ds(i1, 1), :].astype(gdt), (PEER_KEYS, tt))
            gate = gate + jnp.where(r2_ref[h] < na, e1 * e2_ref[h], jnp.zeros((), gdt))
        a = act[j * PEER_KEYS:(j + 1) * PEER_KEYS, :]
        gelu = 0.5 * a * (1.0 + lax.erf(a * math.sqrt(0.5)))
        o_ref[j * PEER_KEYS:(j + 1) * PEER_KEYS, :] = (gate * gelu.astype(gdt)).astype(o_ref.dtype)


def _peer_coef(u_tab, xn, r2, e2, na, e1):
    ne, d = u_tab.shape
    n = xn.shape[0]
    tt = _pick(n, (512, 256, 128))
    big_spec = pl.BlockSpec((PEER_HEADS, PEER_KEYS, tt), lambda i, e: (0, 0, i))
    return pl.pallas_call(
        functools.partial(_coef_body, tt=tt),
        grid=(n // tt, ne // PEER_EB),
        in_specs=[pl.BlockSpec((PEER_EB, d), lambda i, e: (e, 0)),
                  pl.BlockSpec((tt, d), lambda i, e: (i, 0)),
                  big_spec, big_spec, big_spec, big_spec],
        out_specs=pl.BlockSpec((PEER_EB, tt), lambda i, e: (e, i)),
        out_shape=jax.ShapeDtypeStruct((ne, n), BF16),
        compiler_params=_params("parallel", "arbitrary"),
        name="peer_coef",
    )(u_tab, xn, r2, e2, na, e1)


def _round_up(x, m):
    return -(-x // m) * m


def _layer(x, pos0, conv_prev, k_past, v_past, ki_past, mem_k, mem_v, w):
    b, t, d = x.shape
    n = b * t
    ch = d // 2
    n_heads = (d - ch) // HEAD_DIM
    off = w["offsets"]
    pos = pos0 + jnp.arange(t, dtype=jnp.int32)
    rope = _rope_tables(pos)

    x2 = x.reshape(n, d)
    hn = _rmsnorm(x2, w["norm_mix_g"])
    tm = _pick(n, (1024, 512, 256, 128, 64))
    if tm > t:
        rope = tuple(jnp.tile(tab, (tm // t, 1)) for tab in rope)
    u = _proj_glu(hn, w["w_in"], off["a"], off["g"], ch, tm)
    qs = _proj_heads(hn, w["w_in"], off["q"], n_heads * HEAD_DIM, rope, tm, gain=w["q_norm_g"], scale=ATTN_SCALE)
    qi = _proj_heads(hn, w["w_in"], off["qi"], IDX_HEADS * IDX_DIM, rope, tm)
    k32, k16, v32, v16, ki32, ki16, wi = _proj_tail(hn, w["w_in"], off["k"], rope, tm, w["k_norm_g"])
    kvw = KV_W

    u3 = u.reshape(b, t, ch)
    prev = jnp.zeros((b, CONV_STATE, ch), F32) if conv_prev is None else conv_prev
    conv_out = _conv_module(u3, prev, w["dw_w"], w["dw_b"], w["conv_ln_g"], w["conv_ln_b"])
    conv_new = u3[:, t - CONV_STATE:, :]

    k3, v3, ki3 = k16.reshape(b, t, kvw), v16.reshape(b, t, kvw), ki16.reshape(b, t, IDX_DIM)
    if k_past is not None:
        past = k_past.shape[1]
        k3 = jnp.concatenate([k_past.reshape(b, past, kvw).astype(BF16), k3], axis=1)
        v3 = jnp.concatenate([v_past.reshape(b, past, kvw).astype(BF16), v3], axis=1)
        ki3 = jnp.concatenate([ki_past.astype(BF16), ki3], axis=1)
    s_valid = k3.shape[1]
    s_pad = _round_up(s_valid, DSA_SC)
    if s_pad != s_valid:
        padw = ((0, 0), (0, s_pad - s_valid), (0, 0))
        k3, v3, ki3 = jnp.pad(k3, padw), jnp.pad(v3, padw), jnp.pad(ki3, padw)
    topk = min(TOPK_MAX, s_valid // 4)
    attn = _dsa_attention(qs.reshape(b, t, n_heads * HEAD_DIM), k3, v3, qi.reshape(b, t, IDX_HEADS * IDX_DIM),
                          ki3, wi.reshape(b, t, LANES), 0, s_valid, topk, pos0)

    h1 = _matmul_cat(conv_out.reshape(n, ch), attn.reshape(n, n_heads * HEAD_DIM), w["w_out"], x2)

    hn2 = _rmsnorm(h1, w["norm_mem_g"])
    qm = _matmul(hn2, w["w_q_mem"])
    om = _mem_attention(qm.reshape(b, t, MEM_DIM), w["mem_q_norm_g"], mem_k, mem_v)
    h2 = _matmul(om.reshape(n, MEM_DIM), w["w_o_mem"], res=h1)

    hn3 = _rmsnorm(h2, w["norm_ffn_g"])
    pq = _matmul(hn3, w["peer_wq"], out_dtype=BF16)
    r2, e2, na, e1 = _peer_route(pq, w["peer_sub_k1"], w["peer_sub_k2"])
    coef_t = _peer_coef(w["peer_u"], hn3, r2, e2, na, e1)
    ne = coef_t.shape[0]
    y = _matmul(coef_t, w["peer_v"], res=h2, trans_a=True,
                tm=_pick(n, (1024, 512)), tn=_pick(d, (1024, 512)), tk=_pick(ne, (2048, 1024)))

    return (y.reshape(b, t, d), k32.reshape(b, t, N_KV_HEADS, HEAD_DIM), v32.reshape(b, t, N_KV_HEADS, HEAD_DIM),
            ki32.reshape(b, t, IDX_DIM), conv_new)


def _mem_kv(mem, mem_norm_g, w_kv, mem_k_norm_g):
    b, m, d = mem.shape
    mn = _rmsnorm(mem.reshape(b * m, d), mem_norm_g)
    mkv = _matmul(mn, w_kv)
    mk32, mk16 = _head_post(mkv, 0, MEM_HEADS, m, (F32, BF16), gain=mem_k_norm_g)
    mv32 = mkv[:, MEM_DIM:]
    return mk32.reshape(b, m, MEM_DIM), mv32.reshape(b, m, MEM_DIM), mk16.reshape(b, m, MEM_DIM)


def _prep_layer_weights(l, d, norm_mix_g, w_in, dw_w, dw_b, conv_ln_g, conv_ln_b, q_norm_g, k_norm_g, w_out,
                        norm_mem_g, w_q_mem, mem_q_norm_g, w_o_mem, norm_ffn_g, peer_wq, peer_sub_k1,
                        peer_sub_k2, peer_u, peer_v):
    ch = d // 2
    nq = (d - ch)
    kvw = N_KV_HEADS * HEAD_DIM
    niq = IDX_HEADS * IDX_DIM
    src = {"a": 0, "g": ch, "q": 2 * ch}
    src["k"] = src["q"] + nq
    src["v"] = src["k"] + kvw
    src["qi"] = src["v"] + kvw
    src["ki"] = src["qi"] + niq
    src["wi"] = src["ki"] + IDX_DIM
    widths = {"qi": niq, "a": ch, "g": ch, "q": nq, "k": kvw, "v": kvw, "ki": IDX_DIM, "wi": IDX_HEADS}
    order = ["qi", "a", "g", "q", "k", "v", "ki", "wi"]
    wl = w_in[l]
    cols, offsets, o = [], {}, 0
    for name in order:
        if name == "k":
            gap = _round_up(o, TAIL_W) - o
            cols.append(jnp.zeros((d, gap), BF16))
            o += gap
        offsets[name] = o
        cols.append(wl[:, src[name]:src[name] + widths[name]].astype(BF16))
        o += widths[name]
    pad = _round_up(o, LANES) - o
    cols.append(jnp.zeros((d, pad), BF16))
    return {
        "offsets": offsets,
        "w_in": jnp.concatenate(cols, axis=1),
        "norm_mix_g": norm_mix_g[l], "dw_w": dw_w[l], "dw_b": dw_b[l],
        "conv_ln_g": conv_ln_g[l], "conv_ln_b": conv_ln_b[l],
        "q_norm_g": q_norm_g[l], "k_norm_g": k_norm_g[l],
        "w_out": w_out[l].astype(BF16),
        "norm_mem_g": norm_mem_g[l], "w_q_mem": w_q_mem[l].astype(BF16), "mem_q_norm_g": mem_q_norm_g[l],
        "w_o_mem": w_o_mem[l].astype(BF16),
        "norm_ffn_g": norm_ffn_g[l], "peer_wq": peer_wq[l].astype(BF16),
        "peer_sub_k1": peer_sub_k1[l].astype(BF16), "peer_sub_k2": peer_sub_k2[l].astype(BF16),
        "peer_u": peer_u[l].astype(BF16), "peer_v": peer_v[l].astype(BF16),
    }


def kernel(x_prompt, x_sample, mem_prompt, cache_k, cache_v, cache_k_idx, state_conv, cache_mem_k, cache_mem_v, norm_mix_g, w_in, dw_w, dw_b, conv_ln_g, conv_ln_b, q_norm_g, k_norm_g, w_out, norm_mem_g, mem_norm_g, w_q_mem, w_k_mem, w_v_mem, mem_q_norm_g, mem_k_norm_g, w_o_mem, norm_ffn_g, peer_wq, peer_sub_k1, peer_sub_k2, peer_u, peer_v):
    depth = w_in.shape[0]
    d = x_prompt.shape[-1]
    past_len = cache_k.shape[2]
    h_p, h_s = x_prompt, x_sample
    outs = [[] for _ in range(10)]
    for l in range(depth):
        w = _prep_layer_weights(l, d, norm_mix_g, w_in, dw_w, dw_b, conv_ln_g, conv_ln_b, q_norm_g, k_norm_g,
                                w_out, norm_mem_g, w_q_mem, mem_q_norm_g, w_o_mem, norm_ffn_g, peer_wq,
                                peer_sub_k1, peer_sub_k2, peer_u, peer_v)
        w_kv = jnp.concatenate([w_k_mem[l], w_v_mem[l]], axis=1).astype(BF16)
        mk32, mv32, mk16 = _mem_kv(mem_prompt, mem_norm_g[l], w_kv, mem_k_norm_g[l])
        h_p, kp, vp, kip, cp = _layer(h_p, 0, None, None, None, None, mk16, mv32.astype(BF16), w)
        bs, ms = cache_mem_k.shape[1], cache_mem_k.shape[2]
        h_s, ks, vs, kis, cs = _layer(h_s, past_len, state_conv[l], cache_k[l], cache_v[l], cache_k_idx[l],
                                      cache_mem_k[l].reshape(bs, ms, MEM_DIM).astype(BF16),
                                      cache_mem_v[l].reshape(bs, ms, MEM_DIM).astype(BF16), w)
        bp, mp = mem_prompt.shape[0], mem_prompt.shape[1]
        for lst, val in zip(outs, (kp, vp, kip, cp, mk32.reshape(bp, mp, MEM_HEADS, HEAD_DIM),
                                   mv32.reshape(bp, mp, MEM_HEADS, HEAD_DIM), ks, vs, kis, cs)):
            lst.append(val)
    return (h_p, h_s) + tuple(jnp.stack(lst) for lst in outs)
```

```python
import functools
import math

import jax
import jax.numpy as jnp
from jax import lax
from jax.experimental import pallas as pl
from jax.experimental.pallas import tpu as pltpu

F32 = jnp.float32
BF16 = jnp.bfloat16

CHUNK = 64
CONV_WIDTH = 31
CONV_STATE = CONV_WIDTH - 1
HEAD_DIM = 128
N_KV_HEADS = 4
ROPE_DIM = HEAD_DIM // 4
ROPE_THETA = 500000.0
IDX_HEADS = 32
IDX_DIM = 128
IDX_SCALE = (IDX_HEADS ** -0.5) * (IDX_DIM ** -0.5)
TOPK_MAX = 256
ATTN_SCALE = HEAD_DIM ** -0.5
MEM_HEADS = 4
MEM_DIM = MEM_HEADS * HEAD_DIM
PEER_KEYS = 128
PEER_HEADS = 8
PEER_QDIM = 256
PEER_TOPK = 16
EPS = 1e-6

LANES = 128
VMEM_LIMIT_BYTES = 56 * 1024 * 1024

NEG_BIG = -1e30
INT_MIN = -(2 ** 31)

_NT = (((1,), (1,)), ((), ()))
_NN = (((1,), (0,)), ((), ()))
_TN = (((0,), (0,)), ((), ()))


def _params(*sem):
    return pltpu.CompilerParams(dimension_semantics=sem, vmem_limit_bytes=VMEM_LIMIT_BYTES)


def _pick(n, candidates):
    for c in candidates:
        if c <= n and n % c == 0:
            return c
    return n


def _rmsnorm_body(x_ref, g_ref, o_ref):
    x = x_ref[...]
    ms = jnp.mean(x * x, axis=-1, keepdims=True)
    o_ref[...] = (x * lax.rsqrt(ms + EPS) * g_ref[...]).astype(o_ref.dtype)


def _rmsnorm(x, g):
    n, d = x.shape
    tr = _pick(n, (256, 128, 64))
    return pl.pallas_call(
        _rmsnorm_body,
        grid=(n // tr,),
        in_specs=[pl.BlockSpec((tr, d), lambda i: (i, 0)),
                  pl.BlockSpec((1, d), lambda i: (0, 0))],
        out_specs=pl.BlockSpec((tr, d), lambda i: (i, 0)),
        out_shape=jax.ShapeDtypeStruct((n, d), BF16),
        compiler_params=_params("parallel"),
        name="rmsnorm",
    )(x, g.reshape(1, d))


def _mm_body(*refs, nk, dims, has_res):
    a_ref, b_ref = refs[0], refs[1]
    r_ref = refs[2] if has_res else None
    o_ref = refs[2 + has_res]
    p = lax.dot_general(a_ref[...], b_ref[...], dims, preferred_element_type=F32)
    if nk == 1:
        if has_res:
            p = r_ref[...] + p
        o_ref[...] = p.astype(o_ref.dtype)
        return
    k = pl.program_id(2)

    @pl.when(k == 0)
    def _():
        o_ref[...] = r_ref[...] + p if has_res else p

    @pl.when(k > 0)
    def _():
        o_ref[...] += p


def _matmul(a, b, res=None, out_dtype=F32, trans_a=False, tm=None, tn=None, tk=None):
    if trans_a:
        kdim, m = a.shape
    else:
        m, kdim = a.shape
    n = b.shape[1]
    tm = tm or _pick(m, (1024, 512, 256, 128, 64))
    tn = tn or _pick(n, (768, 512, 384, 256, 128))
    tk = tk or kdim
    nk = kdim // tk
    if trans_a:
        a_spec = pl.BlockSpec((tk, tm), lambda i, j, k: (k, i))
        dims = _TN
    else:
        a_spec = pl.BlockSpec((tm, tk), lambda i, j, k: (i, k))
        dims = _NN
    in_specs = [a_spec, pl.BlockSpec((tk, tn), lambda i, j, k: (k, j))]
    args = [a, b]
    if res is not None:
        in_specs.append(pl.BlockSpec((tm, tn), lambda i, j, k: (i, j)))
        args.append(res)
    assert nk == 1 or out_dtype == F32
    return pl.pallas_call(
        functools.partial(_mm_body, nk=nk, dims=dims, has_res=res is not None),
        grid=(m // tm, n // tn, nk),
        in_specs=in_specs,
        out_specs=pl.BlockSpec((tm, tn), lambda i, j, k: (i, j)),
        out_shape=jax.ShapeDtypeStruct((m, n), out_dtype),
        compiler_params=_params("parallel", "parallel", "arbitrary"),
        name="matmul",
    )(*args)


def _mm2_body(a1_ref, a2_ref, b_ref, r_ref, o_ref):
    k1 = a1_ref.shape[1]
    p = lax.dot_general(a1_ref[...], b_ref[0:k1, :], _NN, preferred_element_type=F32)
    p = p + lax.dot_general(a2_ref[...], b_ref[k1:, :], _NN, preferred_element_type=F32)
    o_ref[...] = (r_ref[...] + p).astype(o_ref.dtype)


def _matmul_cat(a1, a2, b, res):
    m, k1 = a1.shape
    k2 = a2.shape[1]
    n = b.shape[1]
    tm = _pick(m, (1024, 512, 256, 128, 64))
    tn = _pick(n, (768, 512, 384, 256, 128))
    return pl.pallas_call(
        _mm2_body,
        grid=(m // tm, n // tn),
        in_specs=[pl.BlockSpec((tm, k1), lambda i, j: (i, 0)),
                  pl.BlockSpec((tm, k2), lambda i, j: (i, 0)),
                  pl.BlockSpec((k1 + k2, tn), lambda i, j: (0, j)),
                  pl.BlockSpec((tm, tn), lambda i, j: (i, j))],
        out_specs=pl.BlockSpec((tm, tn), lambda i, j: (i, j)),
        out_shape=jax.ShapeDtypeStruct((m, n), F32),
        compiler_params=_params("parallel", "parallel"),
        name="matmul_cat",
    )(a1, a2, b, res)


def _head_body(*refs, nh, has_norm, has_rope, scale, n_out):
    it = iter(refs)
    p_ref = next(it)
    g_ref = next(it) if has_norm else None
    if has_rope:
        c = next(it)[...]
        sa = next(it)[...]
        sb = next(it)[...]
    outs = [next(it) for _ in range(n_out)]
    for h in range(nh):
        sl = slice(h * HEAD_DIM, (h + 1) * HEAD_DIM)
        x = p_ref[:, sl]
        if has_norm:
            ms = jnp.mean(x * x, axis=-1, keepdims=True)
            x = x * lax.rsqrt(ms + EPS) * g_ref[...]
        if has_rope:
            x = x * c + pltpu.roll(x, LANES - ROPE_DIM // 2, 1) * sa + pltpu.roll(x, ROPE_DIM // 2, 1) * sb
        if scale != 1.0:
            x = x * scale
        for o in outs:
            o[:, sl] = x.astype(o.dtype)


def _head_post(p, col0, nh, t_len, out_dtypes, gain=None, rope=None, scale=1.0):
    n = p.shape[0]
    w = nh * HEAD_DIM
    assert col0 % w == 0
    tr = _pick(t_len, (256, 128, 64))
    tpb = t_len // tr
    in_specs = [pl.BlockSpec((tr, w), lambda i: (i, col0 // w))]
    args = [p]
    if gain is not None:
        in_specs.append(pl.BlockSpec((1, HEAD_DIM), lambda i: (0, 0)))
        args.append(gain.reshape(1, HEAD_DIM))
    if rope is not None:
        for tab in rope:
            in_specs.append(pl.BlockSpec((tr, LANES), lambda i: (i % tpb, 0)))
            args.append(tab)
    outs = pl.pallas_call(
        functools.partial(_head_body, nh=nh, has_norm=gain is not None, has_rope=rope is not None,
                          scale=scale, n_out=len(out_dtypes)),
        grid=(n // tr,),
        in_specs=in_specs,
        out_specs=[pl.BlockSpec((tr, w), lambda i: (i, 0)) for _ in out_dtypes],
        out_shape=[jax.ShapeDtypeStruct((n, w), dt) for dt in out_dtypes],
        compiler_params=_params("parallel"),
        name="head_post",
    )(*args)
    return outs


def _rope_tables(pos):
    half = ROPE_DIM // 2
    inv_freq = jnp.power(ROPE_THETA, -jnp.arange(half, dtype=F32) / half)
    ang = pos.astype(F32)[:, None] * inv_freq[None, :]
    cos, sin = jnp.cos(ang), jnp.sin(ang)
    t = pos.shape[0]
    zh = jnp.zeros((t, half), F32)
    zr = jnp.zeros((t, HEAD_DIM - ROPE_DIM), F32)
    c = jnp.concatenate([cos, cos, jnp.ones((t, HEAD_DIM - ROPE_DIM), F32)], axis=1)
    sa = jnp.concatenate([-sin, zh, zr], axis=1)
    sb = jnp.concatenate([zh, sin, zr], axis=1)
    return c, sa, sb


def _norm_heads(x, gain):
    ms = jnp.mean(x * x, axis=-1, keepdims=True)
    return x * lax.rsqrt(ms + EPS) * gain


def _rope_heads(x, c, sa, sb):
    return x * c + pltpu.roll(x, LANES - ROPE_DIM // 2, 1) * sa + pltpu.roll(x, ROPE_DIM // 2, 1) * sb


def _proj_heads_body(a_ref, b_ref, g_ref, c_ref, sa_ref, sb_ref, o_ref, *, has_norm, scale):
    p = lax.dot_general(a_ref[...], b_ref[...], _NN, preferred_element_type=F32)
    c, sa, sb = c_ref[...], sa_ref[...], sb_ref[...]
    for h in range(p.shape[1] // HEAD_DIM):
        sl = slice(h * HEAD_DIM, (h + 1) * HEAD_DIM)
        x = p[:, sl]
        if has_norm:
            x = _norm_heads(x, g_ref[...])
        x = _rope_heads(x, c, sa, sb)
        if scale != 1.0:
            x = x * scale
        o_ref[:, sl] = x.astype(o_ref.dtype)


def _proj_heads(a, b, col0, width, rope, tm, gain=None, scale=1.0):
    m, kdim = a.shape
    tn = _pick(width, (1024, 512, 256, 128))
    assert col0 % tn == 0
    tpb = rope[0].shape[0] // tm
    g = jnp.ones((HEAD_DIM,), F32) if gain is None else gain
    tab_spec = pl.BlockSpec((tm, LANES), lambda i, j: (i % tpb, 0))
    return pl.pallas_call(
        functools.partial(_proj_heads_body, has_norm=gain is not None, scale=scale),
        grid=(m // tm, width // tn),
        in_specs=[pl.BlockSpec((tm, kdim), lambda i, j: (i, 0)),
                  pl.BlockSpec((kdim, tn), lambda i, j: (0, col0 // tn + j)),
                  pl.BlockSpec((1, HEAD_DIM), lambda i, j: (0, 0)),
                  tab_spec, tab_spec, tab_spec],
        out_specs=pl.BlockSpec((tm, tn), lambda i, j: (i, j)),
        out_shape=jax.ShapeDtypeStruct((m, width), BF16),
        compiler_params=_params("parallel", "parallel"),
        name="proj_heads",
    )(a, b, g.reshape(1, HEAD_DIM), *rope)


def _proj_glu_body(a_ref, ba_ref, bg_ref, o_ref):
    x = a_ref[...]
    pa = lax.dot_general(x, ba_ref[...], _NN, preferred_element_type=F32)
    pg = lax.dot_general(x, bg_ref[...], _NN, preferred_element_type=F32)
    o_ref[...] = pa * jax.nn.sigmoid(pg)


def _proj_glu(a, b, col_a, col_g, width, tm):
    m, kdim = a.shape
    tn = _pick(width, (512, 256, 128))
    assert col_a % tn == 0 and col_g % tn == 0
    return pl.pallas_call(
        _proj_glu_body,
        grid=(m // tm, width // tn),
        in_specs=[pl.BlockSpec((tm, kdim), lambda i, j: (i, 0)),
                  pl.BlockSpec((kdim, tn), lambda i, j: (0, col_a // tn + j)),
                  pl.BlockSpec((kdim, tn), lambda i, j: (0, col_g // tn + j))],
        out_specs=pl.BlockSpec((tm, tn), lambda i, j: (i, j)),
        out_shape=jax.ShapeDtypeStruct((m, width), F32),
        compiler_params=_params("parallel", "parallel"),
        name="proj_glu",
    )(a, b, b)


KV_W = N_KV_HEADS * HEAD_DIM
TAIL_W = 2 * KV_W + IDX_DIM + LANES


def _proj_tail_body(a_ref, b_ref, g_ref, c_ref, sa_ref, sb_ref, k32_ref, k16_ref, v32_ref, v16_ref,
                    ki32_ref, ki16_ref, wi_ref):
    p = lax.dot_general(a_ref[...], b_ref[...], _NN, preferred_element_type=F32)
    c, sa, sb = c_ref[...], sa_ref[...], sb_ref[...]
    for h in range(N_KV_HEADS):
        sl = slice(h * HEAD_DIM, (h + 1) * HEAD_DIM)
        x = _rope_heads(_norm_heads(p[:, sl], g_ref[...]), c, sa, sb)
        k32_ref[:, h, :] = x
        k16_ref[:, sl] = x.astype(k16_ref.dtype)
    v = p[:, KV_W:2 * KV_W]
    for h in range(N_KV_HEADS):
        v32_ref[:, h, :] = v[:, h * HEAD_DIM:(h + 1) * HEAD_DIM]
    v16_ref[...] = v.astype(v16_ref.dtype)
    ki = _rope_heads(p[:, 2 * KV_W:2 * KV_W + IDX_DIM], c, sa, sb)
    ki32_ref[...] = ki
    ki16_ref[...] = ki.astype(ki16_ref.dtype)
    wi_ref[...] = p[:, 2 * KV_W + IDX_DIM:]


def _proj_tail(a, b, col0, rope, tm, k_gain):
    m, kdim = a.shape
    assert col0 % TAIL_W == 0
    tpb = rope[0].shape[0] // tm
    tab_spec = pl.BlockSpec((tm, LANES), lambda i: (i % tpb, 0))
    heads = (N_KV_HEADS, HEAD_DIM)
    shapes = (heads, (KV_W,), heads, (KV_W,), (IDX_DIM,), (IDX_DIM,), (LANES,))
    dtypes = (F32, BF16, F32, BF16, F32, BF16, F32)
    return pl.pallas_call(
        _proj_tail_body,
        grid=(m // tm,),
        in_specs=[pl.BlockSpec((tm, kdim), lambda i: (i, 0)),
                  pl.BlockSpec((kdim, TAIL_W), lambda i: (0, col0 // TAIL_W)),
                  pl.BlockSpec((1, HEAD_DIM), lambda i: (0, 0)),
                  tab_spec, tab_spec, tab_spec],
        out_specs=[pl.BlockSpec((tm,) + sh, lambda i, nd=len(sh): (i,) + (0,) * nd) for sh in shapes],
        out_shape=[jax.ShapeDtypeStruct((m,) + sh, dt) for sh, dt in zip(shapes, dtypes)],
        compiler_params=_params("parallel"),
        name="proj_tail",
    )(a, b, k_gain.reshape(1, HEAD_DIM), *rope)


CONV_HALO = 32
CONV_CW = 256
CONV_TB = 64
SUBLANES = 8


def _conv_body(u_ref, prev_ref, w_ref, b_ref, g_ref, beta_ref, o_ref, win_ref, rot_ref, c_ref, *, tt, ch):
    @pl.when(pl.program_id(1) == 0)
    def _():
        win_ref[0:CONV_HALO, :] = prev_ref[0]

    win_ref[CONV_HALO:CONV_HALO + tt, :] = u_ref[0]
    base = CONV_HALO - CONV_STATE
    for cc in range(ch // CONV_CW):
        cs = slice(cc * CONV_CW, (cc + 1) * CONV_CW)
        for r in range(SUBLANES):
            rows = SUBLANES * ((CONV_WIDTH - 1 - r) // SUBLANES) + tt
            rot_ref[r, 0:rows, :] = win_ref[base + r:base + r + rows, cs]
        for tb in range(tt // CONV_TB):
            acc = None
            for j in range(CONV_WIDTH):
                q, r = divmod(j, SUBLANES)
                lo = SUBLANES * q + tb * CONV_TB
                term = rot_ref[r, lo:lo + CONV_TB, :] * w_ref[j:j + 1, cs]
                acc = term if acc is None else acc + term
            c_ref[tb * CONV_TB:(tb + 1) * CONV_TB, cs] = acc + b_ref[:, cs]
    c = c_ref[...]
    mu = jnp.mean(c, axis=-1, keepdims=True)
    d = c - mu
    var = jnp.mean(d * d, axis=-1, keepdims=True)
    y = d * lax.rsqrt(var + EPS) * g_ref[...] + beta_ref[...]
    o_ref[0] = (y * jax.nn.sigmoid(y)).astype(o_ref.dtype)
    win_ref[0:CONV_HALO, :] = win_ref[tt:tt + CONV_HALO, :]


def _conv_module(u, prev, dw_w, dw_b, ln_g, ln_b):
    b, t, ch = u.shape
    tt = _pick(t, (256, 128, 64))
    prev_p = jnp.pad(prev, ((0, 0), (CONV_HALO - CONV_STATE, 0), (0, 0)))
    w_p = jnp.pad(dw_w, ((0, CONV_HALO - CONV_WIDTH), (0, 0)))
    row = lambda v: v.reshape(1, ch)
    rot_rows = SUBLANES * ((CONV_WIDTH - 1) // SUBLANES) + tt
    return pl.pallas_call(
        functools.partial(_conv_body, tt=tt, ch=ch),
        grid=(b, t // tt),
        in_specs=[pl.BlockSpec((1, tt, ch), lambda i, j: (i, j, 0)),
                  pl.BlockSpec((1, CONV_HALO, ch), lambda i, j: (i, 0, 0)),
                  pl.BlockSpec((CONV_HALO, ch), lambda i, j: (0, 0)),
                  pl.BlockSpec((1, ch), lambda i, j: (0, 0)),
                  pl.BlockSpec((1, ch), lambda i, j: (0, 0)),
                  pl.BlockSpec((1, ch), lambda i, j: (0, 0))],
        out_specs=pl.BlockSpec((1, tt, ch), lambda i, j: (i, j, 0)),
        out_shape=jax.ShapeDtypeStruct((b, t, ch), BF16),
        scratch_shapes=[pltpu.VMEM((tt + CONV_HALO, ch), F32),
                        pltpu.VMEM((SUBLANES, rot_rows, CONV_CW), F32),
                        pltpu.VMEM((tt, ch), F32)],
        compiler_params=_params("arbitrary", "arbitrary"),
        name="conv_module",
    )(u, prev_p, w_p, row(dw_b), row(ln_g), row(ln_b))


DSA_SC = 512


def _dsa_body(qs_ref, k_ref, v_ref, qi_ref, ki_ref, wi_ref, o_ref, key_ref, bias_ref, thr_ref, jcut_ref,
              qst_ref, mx_ref, acc_ref, *, tq, nchunks, buckets, s_valid, topk, pos0, n_heads):
    sc = DSA_SC
    group = n_heads // N_KV_HEADS
    rows = group * tq
    q_lo = pos0 + pl.program_id(1) * tq
    qpos = q_lo + lax.broadcasted_iota(jnp.int32, (tq, 1), 0)
    chunk_shift = CHUNK.bit_length() - 1
    qchunk = qpos >> chunk_shift
    k_end = jnp.minimum((((q_lo + tq - 1) >> chunk_shift) + 1) << chunk_shift, s_valid)
    nact = jnp.minimum((k_end + sc - 1) >> (sc.bit_length() - 1), nchunks)
    nbucket = jnp.int32(buckets[-1])
    for bsz in reversed(buckets[:-1]):
        nbucket = jnp.where(nact <= bsz, bsz, nbucket)

    def score_chunk(c, carry):
        off = pl.multiple_of(c * sc, sc)
        ki_c = ki_ref[0, pl.ds(off, sc), :]
        acc = jnp.zeros((tq, sc), F32)
        for h in range(IDX_HEADS):
            s = lax.dot_general(qi_ref[0, :, h * IDX_DIM:(h + 1) * IDX_DIM], ki_c, _NT,
                                preferred_element_type=F32)
            acc = acc + wi_ref[0, :, h:h + 1] * jnp.maximum(s, 0.0)
        score = acc * IDX_SCALE
        spos = off + lax.broadcasted_iota(jnp.int32, (1, sc), 1)
        schunk = spos >> chunk_shift
        adm = jnp.logical_and(schunk <= qchunk, spos < s_valid)
        bits = pltpu.bitcast(score, jnp.int32)
        key = bits ^ ((bits >> 31) & 0x7FFFFFFF)
        key_ref[c] = jnp.where(adm, key, INT_MIN)
        return carry

    lax.fori_loop(0, nact, score_chunk, 0)

    def fill_chunk(c, carry):
        key_ref[c] = jnp.full((tq, sc), INT_MIN, jnp.int32)
        return carry

    lax.fori_loop(nact, nbucket, fill_chunk, 0)

    for bsz in buckets:
        @pl.when(nbucket == bsz)
        def _(bsz=bsz):
            def count(mask):
                return jnp.sum(jnp.sum(mask.astype(F32), axis=0), axis=-1, keepdims=True)

            def count_ge(cand):
                return count(key_ref[0:bsz] >= cand[None])

            zero = jnp.zeros((tq, 1), jnp.int32)
            ans = jnp.where(count_ge(zero) >= topk, zero, INT_MIN)

            def bit_step(i, ans):
                cand = ans | lax.shift_left(jnp.int32(1), 30 - i)
                return jnp.where(count_ge(cand) >= topk, cand, ans)

            ans = lax.fori_loop(0, 31, bit_step, ans)
            thr = jnp.maximum(ans, INT_MIN + 1)
            thr_ref[...] = thr
            need = topk - count(key_ref[0:bsz] > thr[None])
            ties = count(key_ref[0:bsz] == thr[None])
            nbits = (bsz * sc).bit_length()
            jcut_ref[...] = jnp.full((tq, 1), (1 << nbits) - 1, jnp.int32)

            @pl.when(jnp.max(ties - need) > 0.0)
            def _():
                pos = (lax.broadcasted_iota(jnp.int32, (bsz, tq, sc), 0) * sc
                       + lax.broadcasted_iota(jnp.int32, (bsz, tq, sc), 2))

                def ties_before(j):
                    keys = key_ref[0:bsz]
                    return count(jnp.where(keys == thr_ref[...][None], pos, (1 << nbits)) < j[None])

                def pos_step(i, j):
                    cand = j | lax.shift_left(jnp.int32(1), nbits - 1 - i)
                    return jnp.where(ties_before(cand) <= need, cand, j)

                jcut_ref[...] = lax.fori_loop(0, nbits, pos_step, zero)

    thr = thr_ref[...]
    jcut = jcut_ref[...]

    def bias_chunk(c, carry):
        spos = c * sc + lax.broadcasted_iota(jnp.int32, (1, sc), 1)
        bound = thr - jnp.where(spos < jcut, 1, 0)
        bias_ref[c] = jnp.where(key_ref[c] > bound, 0.0, NEG_BIG)
        return carry

    lax.fori_loop(0, nact, bias_chunk, 0)

    for hd in range(n_heads):
        qst_ref[hd * tq:(hd + 1) * tq, :] = qs_ref[0, :, hd * HEAD_DIM:(hd + 1) * HEAD_DIM]
    ntile = sc // LANES

    def masked_logits(c, g):
        off = pl.multiple_of(c * sc, sc)
        k_c = k_ref[0, pl.ds(off, sc), g * HEAD_DIM:(g + 1) * HEAD_DIM]
        lg = lax.dot_general(qst_ref[g * rows:(g + 1) * rows, :], k_c, _NT, preferred_element_type=F32)
        return (lg.reshape(group, tq, sc) + bias_ref[c][None]).reshape(rows, sc)

    mx_ref[...] = jnp.full(mx_ref.shape, NEG_BIG, F32)

    def max_chunk(c, carry):
        for g in range(N_KV_HEADS):
            rs = slice(g * rows, (g + 1) * rows)
            lg = masked_logits(c, g)
            mx = mx_ref[rs, :]
            for t in range(ntile):
                mx = jnp.maximum(mx, lg[:, t * LANES:(t + 1) * LANES])
            mx_ref[rs, :] = mx
        return carry

    lax.fori_loop(0, nact, max_chunk, 0)
    mx_ref[...] = jnp.broadcast_to(jnp.max(mx_ref[...], axis=-1, keepdims=True), mx_ref.shape)

    acc_ref[...] = jnp.zeros(acc_ref.shape, F32)
    ones = jnp.ones((sc, HEAD_DIM), BF16)

    def attn_chunk(c, carry):
        off = pl.multiple_of(c * sc, sc)
        for g in range(N_KV_HEADS):
            rs = slice(g * rows, (g + 1) * rows)
            lg = masked_logits(c, g)
            mb = mx_ref[rs, :]
            p = jnp.concatenate([jnp.exp(lg[:, t * LANES:(t + 1) * LANES] - mb) for t in range(ntile)], axis=1)
            v_ext = jnp.concatenate([v_ref[0, pl.ds(off, sc), g * HEAD_DIM:(g + 1) * HEAD_DIM], ones], axis=1)
            acc_ref[rs, :] += lax.dot_general(p.astype(BF16), v_ext, _NN, preferred_element_type=F32)
        return carry

    lax.fori_loop(0, nact, attn_chunk, 0)
    for hd in range(n_heads):
        hs = slice(hd * tq, (hd + 1) * tq)
        o = acc_ref[hs, 0:HEAD_DIM] / acc_ref[hs, HEAD_DIM:2 * HEAD_DIM]
        o_ref[0, :, hd * HEAD_DIM:(hd + 1) * HEAD_DIM] = o.astype(o_ref.dtype)


def _dsa_attention(qs, k_all, v_all, qi, ki_all, p3, wi_col, s_valid, topk, pos0):
    b, t, qw = qs.shape
    s_pad = k_all.shape[1]
    n_heads = qw // HEAD_DIM
    tq = _pick(t, (128, 64))
    nchunks = s_pad // DSA_SC
    if pos0 + tq >= s_valid:
        buckets = (nchunks,)
    else:
        buckets = tuple(sorted({min(1 << e, nchunks) for e in range(nchunks.bit_length() + 1)}))
    return pl.pallas_call(
        functools.partial(_dsa_body, tq=tq, nchunks=nchunks, buckets=buckets, s_valid=s_valid, topk=topk,
                          pos0=pos0, n_heads=n_heads),
        grid=(b, t // tq),
        in_specs=[pl.BlockSpec((1, tq, qw), lambda i, j: (i, j, 0)),
                  pl.BlockSpec((1, s_pad, N_KV_HEADS * HEAD_DIM), lambda i, j: (i, 0, 0)),
                  pl.BlockSpec((1, s_pad, N_KV_HEADS * HEAD_DIM), lambda i, j: (i, 0, 0)),
                  pl.BlockSpec((1, tq, IDX_HEADS * IDX_DIM), lambda i, j: (i, j, 0)),
                  pl.BlockSpec((1, s_pad, IDX_DIM), lambda i, j: (i, 0, 0)),
                  pl.BlockSpec((1, tq, LANES), lambda i, j: (i, j, wi_col // LANES))],
        out_specs=pl.BlockSpec((1, tq, qw), lambda i, j: (i, j, 0)),
        out_shape=jax.ShapeDtypeStruct((b, t, qw), BF16),
        scratch_shapes=[pltpu.VMEM((nchunks, tq, DSA_SC), jnp.int32),
                        pltpu.VMEM((nchunks, tq, DSA_SC), F32),
                        pltpu.VMEM((tq, 1), jnp.int32),
                        pltpu.VMEM((tq, 1), jnp.int32),
                        pltpu.VMEM((n_heads * tq, HEAD_DIM), BF16),
                        pltpu.VMEM((n_heads * tq, LANES), F32),
                        pltpu.VMEM((n_heads * tq, 2 * HEAD_DIM), F32)],
        compiler_params=_params("parallel", "arbitrary"),
        name="dsa_attention",
    )(qs, k_all, v_all, qi, ki_all, p3)


def _memattn_body(q_ref, g_ref, mk_ref, mv_ref, o_ref):
    for h in range(MEM_HEADS):
        sl = slice(h * HEAD_DIM, (h + 1) * HEAD_DIM)
        x = q_ref[0, :, sl]
        ms = jnp.mean(x * x, axis=-1, keepdims=True)
        qn = (x * lax.rsqrt(ms + EPS) * g_ref[...] * ATTN_SCALE).astype(BF16)
        lg = lax.dot_general(qn, mk_ref[0, :, sl], _NT, preferred_element_type=F32)
        m = jnp.max(lg, axis=-1, keepdims=True)
        p = jnp.exp(lg - m)
        l = jnp.sum(p, axis=-1, keepdims=True)
        o = lax.dot_general(p.astype(BF16), mv_ref[0, :, sl], _NN, preferred_element_type=F32)
        o_ref[0, :, sl] = (o / l).astype(o_ref.dtype)


def _mem_attention(qm, gain, mk, mv):
    b, t, w = qm.shape
    m = mk.shape[1]
    tq = _pick(t, (512, 256, 128, 64))
    return pl.pallas_call(
        _memattn_body,
        grid=(b, t // tq),
        in_specs=[pl.BlockSpec((1, tq, w), lambda i, j: (i, j, 0)),
                  pl.BlockSpec((1, HEAD_DIM), lambda i, j: (0, 0)),
                  pl.BlockSpec((1, m, w), lambda i, j: (i, 0, 0)),
                  pl.BlockSpec((1, m, w), lambda i, j: (i, 0, 0))],
        out_specs=pl.BlockSpec((1, tq, w), lambda i, j: (i, j, 0)),
        out_shape=jax.ShapeDtypeStruct((b, t, w), BF16),
        compiler_params=_params("parallel", "parallel"),
        name="mem_attention",
    )(qm, gain.reshape(1, HEAD_DIM), mk, mv)


PEER_HALF = PEER_QDIM // 2
_PEER_PAIRS = [(a, b) for a in range(PEER_TOPK) for b in range(PEER_TOPK) if (a + 1) * (b + 1) <= PEER_TOPK]


def _route_body(pq_ref, k1_ref, k2_ref, r2_ref, e2_ref, na_ref, e1_ref, rk1_ref, v1_ref, v2_ref, c_ref, *, tt):
    row = lax.broadcasted_iota(jnp.int32, (PEER_KEYS, tt), 0)
    for h in range(PEER_HEADS):
        for table, (kref, vref, lo) in enumerate(((k1_ref, v1_ref, 0), (k2_ref, v2_ref, PEER_HALF))):
            q = pq_ref[:, h * PEER_QDIM + lo:h * PEER_QDIM + lo + PEER_HALF]
            s = lax.dot_general(kref[h], q, _NT, preferred_element_type=F32)
            cur = s
            rank = jnp.full((PEER_KEYS, tt), float(PEER_TOPK), F32)
            for r in range(PEER_TOPK):
                m = jnp.max(cur, axis=0, keepdims=True)
                vref[r, h:h + 1, :] = m
                first = jnp.min(jnp.where(cur == m, row, PEER_KEYS), axis=0, keepdims=True)
                hit = row == first
                cur = jnp.where(hit, -jnp.inf, cur)
                rank = jnp.where(hit, float(r), rank)
            ex = jnp.exp(s - vref[0, h:h + 1, :])
            if table == 0:
                rk1_ref[h] = rank
                e1_ref[h] = ex
            else:
                r2_ref[h] = rank.astype(r2_ref.dtype)
                e2_ref[h] = ex.astype(e2_ref.dtype)
    for p, (a, b) in enumerate(_PEER_PAIRS):
        c_ref[p] = v1_ref[a] + v2_ref[b]
    call = c_ref[...]

    def tau_step(p, tau):
        cp = c_ref[p]
        cnt = jnp.sum((call >= cp[None]).astype(F32), axis=0)
        return jnp.maximum(tau, jnp.where(cnt >= PEER_TOPK, cp, -jnp.inf))

    tau = lax.fori_loop(0, len(_PEER_PAIRS), tau_step, jnp.full((PEER_HEADS, tt), -jnp.inf, F32))
    zero = jnp.zeros((PEER_HEADS, tt), F32)
    cnt_gt = [zero] * PEER_TOPK
    cnt_eq = [zero] * PEER_TOPK
    for p, (a, b) in enumerate(_PEER_PAIRS):
        cnt_gt[a] = cnt_gt[a] + (call[p] > tau).astype(F32)
        cnt_eq[a] = cnt_eq[a] + (call[p] == tau).astype(F32)
    rem = float(PEER_TOPK) - sum(cnt_gt)
    n_of = []
    for a in range(PEER_TOPK):
        take = jnp.minimum(cnt_eq[a], rem)
        rem = rem - take
        n_of.append(cnt_gt[a] + take)
    z = zero
    for p, (a, b) in enumerate(_PEER_PAIRS):
        z = z + jnp.where(n_of[a] > float(b), jnp.exp(call[p] - call[0]), 0.0)
    inv_z = 1.0 / z
    for h in range(PEER_HEADS):
        rk = rk1_ref[h]
        na = jnp.zeros((PEER_KEYS, tt), F32)
        for a in range(PEER_TOPK):
            na = jnp.where(rk == float(a), n_of[a][h:h + 1, :], na)
        na_ref[h] = na
        e1_ref[h] = e1_ref[h] * inv_z[h:h + 1, :]


def _peer_route(pq, k1, k2):
    n = pq.shape[0]
    tt = _pick(n, (256, 128))
    shape = (PEER_HEADS, PEER_KEYS, n)
    big_spec = pl.BlockSpec((PEER_HEADS, PEER_KEYS, tt), lambda i: (0, 0, i))
    kspec = pl.BlockSpec((PEER_HEADS, PEER_KEYS, PEER_HALF), lambda i: (0, 0, 0))
    return pl.pallas_call(
        functools.partial(_route_body, tt=tt),
        grid=(n // tt,),
        in_specs=[pl.BlockSpec((tt, PEER_HEADS * PEER_QDIM), lambda i: (i, 0)), kspec, kspec],
        out_specs=[big_spec, big_spec, big_spec, big_spec],
        out_shape=[jax.ShapeDtypeStruct(shape, BF16), jax.ShapeDtypeStruct(shape, BF16),
                   jax.ShapeDtypeStruct(shape, F32), jax.ShapeDtypeStruct(shape, F32)],
        scratch_shapes=[pltpu.VMEM((PEER_HEADS, PEER_KEYS, tt), F32),
                        pltpu.VMEM((PEER_TOPK, PEER_HEADS, tt), F32),
                        pltpu.VMEM((PEER_TOPK, PEER_HEADS, tt), F32),
                        pltpu.VMEM((len(_PEER_PAIRS), PEER_HEADS, tt), F32)],
        compiler_params=_params("parallel"),
        name="peer_route",
    )(pq, k1, k2)


PEER_EB = 1024


def _coef_body(u_ref, x_ref, r2_ref, e2_ref, na_ref, e1_ref, o_ref, *, tt):
    nb = PEER_EB // PEER_KEYS
    e = pl.program_id(1)
    act = lax.dot_general(u_ref[...], x_ref[...], _NT, preferred_element_type=F32)
    gdt = r2_ref.dtype
    for j in range(nb):
        i1 = e * nb + j
        gate = jnp.zeros((PEER_KEYS, tt), gdt)
        for h in range(PEER_HEADS):
            na = jnp.broadcast_to(na_ref[h, pl.ds(i1, 1), :].astype(gdt), (PEER_KEYS, tt))
            e1 = jnp.broadcast_to(e1_ref[h, pl.ds(i1, 1), :].astype(gdt), (PEER_KEYS, tt))
            gate = gate + jnp.where(r2_ref[h] < na, e1 * e2_ref[h], jnp.zeros((), gdt))
        a = act[j * PEER_KEYS:(j + 1) * PEER_KEYS, :]
        gelu = 0.5 * a * (1.0 + lax.erf(a * math.sqrt(0.5)))
        o_ref[j * PEER_KEYS:(j + 1) * PEER_KEYS, :] = (gate * gelu.astype(gdt)).astype(o_ref.dtype)


def _peer_coef(u_tab, xn, r2, e2, na, e1):
    ne, d = u_tab.shape
    n = xn.shape[0]
    tt = _pick(n, (512, 256, 128))
    big_spec = pl.BlockSpec((PEER_HEADS, PEER_KEYS, tt), lambda i, e: (0, 0, i))
    return pl.pallas_call(
        functools.partial(_coef_body, tt=tt),
        grid=(n // tt, ne // PEER_EB),
        in_specs=[pl.BlockSpec((PEER_EB, d), lambda i, e: (e, 0)),
                  pl.BlockSpec((tt, d), lambda i, e: (i, 0)),
                  big_spec, big_spec, big_spec, big_spec],
        out_specs=pl.BlockSpec((PEER_EB, tt), lambda i, e: (e, i)),
        out_shape=jax.ShapeDtypeStruct((ne, n), BF16),
        compiler_params=_params("parallel", "arbitrary"),
        name="peer_coef",
    )(u_tab, xn, r2, e2, na, e1)


def _round_up(x, m):
    return -(-x // m) * m


def _layer(x, pos0, conv_prev, k_past, v_past, ki_past, mem_k, mem_v, w):
    b, t, d = x.shape
    n = b * t
    ch = d // 2
    n_heads = (d - ch) // HEAD_DIM
    off = w["offsets"]
    pos = pos0 + jnp.arange(t, dtype=jnp.int32)
    rope = _rope_tables(pos)

    x2 = x.reshape(n, d)
    hn = _rmsnorm(x2, w["norm_mix_g"])
    tm = _pick(n, (1024, 512, 256, 128, 64))
    if tm > t:
        rope = tuple(jnp.tile(tab, (tm // t, 1)) for tab in rope)
    u = _proj_glu(hn, w["w_in"], off["a"], off["g"], ch, tm)
    qs = _proj_heads(hn, w["w_in"], off["q"], n_heads * HEAD_DIM, rope, tm, gain=w["q_norm_g"], scale=ATTN_SCALE)
    qi = _proj_heads(hn, w["w_in"], off["qi"], IDX_HEADS * IDX_DIM, rope, tm)
    k32, k16, v32, v16, ki32, ki16, wi = _proj_tail(hn, w["w_in"], off["k"], rope, tm, w["k_norm_g"])
    kvw = KV_W

    u3 = u.reshape(b, t, ch)
    prev = jnp.zeros((b, CONV_STATE, ch), F32) if conv_prev is None else conv_prev
    conv_out = _conv_module(u3, prev, w["dw_w"], w["dw_b"], w["conv_ln_g"], w["conv_ln_b"])
    conv_new = u3[:, t - CONV_STATE:, :]

    k3, v3, ki3 = k16.reshape(b, t, kvw), v16.reshape(b, t, kvw), ki16.reshape(b, t, IDX_DIM)
    if k_past is not None:
        past = k_past.shape[1]
        k3 = jnp.concatenate([k_past.reshape(b, past, kvw).astype(BF16), k3], axis=1)
        v3 = jnp.concatenate([v_past.reshape(b, past, kvw).astype(BF16), v3], axis=1)
        ki3 = jnp.concatenate([ki_past.astype(BF16), ki3], axis=1)
    s_valid = k3.shape[1]
    s_pad = _round_up(s_valid, DSA_SC)
    if s_pad != s_valid:
        padw = ((0, 0), (0, s_pad - s_valid), (0, 0))
        k3, v3, ki3 = jnp.pad(k3, padw), jnp.pad(v3, padw), jnp.pad(ki3, padw)
    topk = min(TOPK_MAX, s_valid // 4)
    attn = _dsa_attention(qs.reshape(b, t, n_heads * HEAD_DIM), k3, v3, qi.reshape(b, t, IDX_HEADS * IDX_DIM),
                          ki3, wi.reshape(b, t, LANES), 0, s_valid, topk, pos0)

    h1 = _matmul_cat(conv_out.reshape(n, ch), attn.reshape(n, n_heads * HEAD_DIM), w["w_out"], x2)

    hn2 = _rmsnorm(h1, w["norm_mem_g"])
    qm = _matmul(hn2, w["w_q_mem"])
    om = _mem_attention(qm.reshape(b, t, MEM_DIM), w["mem_q_norm_g"], mem_k, mem_v)
    h2 = _matmul(om.reshape(n, MEM_DIM), w["w_o_mem"], res=h1)

    hn3 = _rmsnorm(h2, w["norm_ffn_g"])
    pq = _matmul(hn3, w["peer_wq"], out_dtype=BF16)
    r2, e2, na, e1 = _peer_route(pq, w["peer_sub_k1"], w["peer_sub_k2"])
    coef_t = _peer_coef(w["peer_u"], hn3, r2, e2, na, e1)
    ne = coef_t.shape[0]
    y = _matmul(coef_t, w["peer_v"], res=h2, trans_a=True,
                tm=_pick(n, (1024, 512)), tn=_pick(d, (1024, 512)), tk=_pick(ne, (2048, 1024)))

    return (y.reshape(b, t, d), k32.reshape(b, t, N_KV_HEADS, HEAD_DIM), v32.reshape(b, t, N_KV_HEADS, HEAD_DIM),
            ki32.reshape(b, t, IDX_DIM), conv_new)


def _mem_kv(mem, mem_norm_g, w_kv, mem_k_norm_g):
    b, m, d = mem.shape
    mn = _rmsnorm(mem.reshape(b * m, d), mem_norm_g)
    mkv = _matmul(mn, w_kv)
    mk32, mk16 = _head_post(mkv, 0, MEM_HEADS, m, (F32, BF16), gain=mem_k_norm_g)
    mv32 = mkv[:, MEM_DIM:]
    return mk32.reshape(b, m, MEM_DIM), mv32.reshape(b, m, MEM_DIM), mk16.reshape(b, m, MEM_DIM)


def _prep_layer_weights(l, d, norm_mix_g, w_in, dw_w, dw_b, conv_ln_g, conv_ln_b, q_norm_g, k_norm_g, w_out,
                        norm_mem_g, w_q_mem, mem_q_norm_g, w_o_mem, norm_ffn_g, peer_wq, peer_sub_k1,
                        peer_sub_k2, peer_u, peer_v):
    ch = d // 2
    nq = (d - ch)
    kvw = N_KV_HEADS * HEAD_DIM
    niq = IDX_HEADS * IDX_DIM
    src = {"a": 0, "g": ch, "q": 2 * ch}
    src["k"] = src["q"] + nq
    src["v"] = src["k"] + kvw
    src["qi"] = src["v"] + kvw
    src["ki"] = src["qi"] + niq
    src["wi"] = src["ki"] + IDX_DIM
    widths = {"qi": niq, "a": ch, "g": ch, "q": nq, "k": kvw, "v": kvw, "ki": IDX_DIM, "wi": IDX_HEADS}
    order = ["qi", "a", "g", "q", "k", "v", "ki", "wi"]
    wl = w_in[l]
    cols, offsets, o = [], {}, 0
    for name in order:
        if name == "k":
            gap = _round_up(o, TAIL_W) - o
            cols.append(jnp.zeros((d, gap), BF16))
            o += gap
        offsets[name] = o
        cols.append(wl[:, src[name]:src[name] + widths[name]].astype(BF16))
        o += widths[name]
    pad = _round_up(o, LANES) - o
    cols.append(jnp.zeros((d, pad), BF16))
    return {
        "offsets": offsets,
        "w_in": jnp.concatenate(cols, axis=1),
        "norm_mix_g": norm_mix_g[l], "dw_w": dw_w[l], "dw_b": dw_b[l],
        "conv_ln_g": conv_ln_g[l], "conv_ln_b": conv_ln_b[l],
        "q_norm_g": q_norm_g[l], "k_norm_g": k_norm_g[l],
        "w_out": w_out[l].astype(BF16),
        "norm_mem_g": norm_mem_g[l], "w_q_mem": w_q_mem[l].astype(BF16), "mem_q_norm_g": mem_q_norm_g[l],
        "w_o_mem": w_o_mem[l].astype(BF16),
        "norm_ffn_g": norm_ffn_g[l], "peer_wq": peer_wq[l].astype(BF16),
        "peer_sub_k1": peer_sub_k1[l].astype(BF16), "peer_sub_k2": peer_sub_k2[l].astype(BF16),
        "peer_u": peer_u[l].astype(BF16), "peer_v": peer_v[l].astype(BF16),
    }


def kernel(x_prompt, x_sample, mem_prompt, cache_k, cache_v, cache_k_idx, state_conv, cache_mem_k, cache_mem_v, norm_mix_g, w_in, dw_w, dw_b, conv_ln_g, conv_ln_b, q_norm_g, k_norm_g, w_out, norm_mem_g, mem_norm_g, w_q_mem, w_k_mem, w_v_mem, mem_q_norm_g, mem_k_norm_g, w_o_mem, norm_ffn_g, peer_wq, peer_sub_k1, peer_sub_k2, peer_u, peer_v):
    depth = w_in.shape[0]
    d = x_prompt.shape[-1]
    past_len = cache_k.shape[2]
    h_p, h_s = x_prompt, x_sample
    outs = [[] for _ in range(10)]
    for l in range(depth):
        w = _prep_layer_weights(l, d, norm_mix_g, w_in, dw_w, dw_b, conv_ln_g, conv_ln_b, q_norm_g, k_norm_g,
                                w_out, norm_mem_g, w_q_mem, mem_q_norm_g, w_o_mem, norm_ffn_g, peer_wq,
                                peer_sub_k1, peer_sub_k2, peer_u, peer_v)
        w_kv = jnp.concatenate([w_k_mem[l], w_v_mem[l]], axis=1).astype(BF16)
        mk32, mv32, mk16 = _mem_kv(mem_prompt, mem_norm_g[l], w_kv, mem_k_norm_g[l])
        h_p, kp, vp, kip, cp = _layer(h_p, 0, None, None, None, None, mk16, mv32.astype(BF16), w)
        bs, ms = cache_mem_k.shape[1], cache_mem_k.shape[2]
        h_s, ks, vs, kis, cs = _layer(h_s, past_len, state_conv[l], cache_k[l], cache_v[l], cache_k_idx[l],
                                      cache_mem_k[l].reshape(bs, ms, MEM_DIM).astype(BF16),
                                      cache_mem_v[l].reshape(bs, ms, MEM_DIM).astype(BF16), w)
        bp, mp = mem_prompt.shape[0], mem_prompt.shape[1]
        for lst, val in zip(outs, (kp, vp, kip, cp, mk32.reshape(bp, mp, MEM_HEADS, HEAD_DIM),
                                   mv32.reshape(bp, mp, MEM_HEADS, HEAD_DIM), ks, vs, kis, cs)):
            lst.append(val)
    return (h_p, h_s) + tuple(jnp.stack(lst) for lst in outs)
```

```python
import functools
import math

import jax
import jax.numpy as jnp
from jax import lax
from jax.experimental import pallas as pl
from jax.experimental.pallas import tpu as pltpu

F32 = jnp.float32
BF16 = jnp.bfloat16

CHUNK = 64
CONV_WIDTH = 31
CONV_STATE = CONV_WIDTH - 1
HEAD_DIM = 128
N_KV_HEADS = 4
ROPE_DIM = HEAD_DIM // 4
ROPE_THETA = 500000.0
IDX_HEADS = 32
IDX_DIM = 128
IDX_SCALE = (IDX_HEADS ** -0.5) * (IDX_DIM ** -0.5)
TOPK_MAX = 256
ATTN_SCALE = HEAD_DIM ** -0.5
MEM_HEADS = 4
MEM_DIM = MEM_HEADS * HEAD_DIM
PEER_KEYS = 128
PEER_HEADS = 8
PEER_QDIM = 256
PEER_TOPK = 16
EPS = 1e-6

LANES = 128
VMEM_LIMIT_BYTES = 56 * 1024 * 1024

NEG_BIG = -1e30
INT_MIN = -(2 ** 31)

_NT = (((1,), (1,)), ((), ()))
_NN = (((1,), (0,)), ((), ()))
_TN = (((0,), (0,)), ((), ()))


def _params(*sem):
    return pltpu.CompilerParams(dimension_semantics=sem, vmem_limit_bytes=VMEM_LIMIT_BYTES)


def _pick(n, candidates):
    for c in candidates:
        if c <= n and n % c == 0:
            return c
    return n


def _rmsnorm_body(x_ref, g_ref, o_ref):
    x = x_ref[...]
    ms = jnp.mean(x * x, axis=-1, keepdims=True)
    o_ref[...] = (x * lax.rsqrt(ms + EPS) * g_ref[...]).astype(o_ref.dtype)


def _rmsnorm(x, g):
    n, d = x.shape
    tr = _pick(n, (256, 128, 64))
    return pl.pallas_call(
        _rmsnorm_body,
        grid=(n // tr,),
        in_specs=[pl.BlockSpec((tr, d), lambda i: (i, 0)),
                  pl.BlockSpec((1, d), lambda i: (0, 0))],
        out_specs=pl.BlockSpec((tr, d), lambda i: (i, 0)),
        out_shape=jax.ShapeDtypeStruct((n, d), BF16),
        compiler_params=_params("parallel"),
        name="rmsnorm",
    )(x, g.reshape(1, d))


def _mm_body(*refs, nk, dims, has_res):
    a_ref, b_ref = refs[0], refs[1]
    r_ref = refs[2] if has_res else None
    o_ref = refs[2 + has_res]
    p = lax.dot_general(a_ref[...], b_ref[...], dims, preferred_element_type=F32)
    if nk == 1:
        if has_res:
            p = r_ref[...] + p
        o_ref[...] = p.astype(o_ref.dtype)
        return
    k = pl.program_id(2)

    @pl.when(k == 0)
    def _():
        o_ref[...] = r_ref[...] + p if has_res else p

    @pl.when(k > 0)
    def _():
        o_ref[...] += p


def _matmul(a, b, res=None, out_dtype=F32, trans_a=False, tm=None, tn=None, tk=None):
    if trans_a:
        kdim, m = a.shape
    else:
        m, kdim = a.shape
    n = b.shape[1]
    tm = tm or _pick(m, (1024, 512, 256, 128, 64))
    tn = tn or _pick(n, (768, 512, 384, 256, 128))
    tk = tk or kdim
    nk = kdim // tk
    if trans_a:
        a_spec = pl.BlockSpec((tk, tm), lambda i, j, k: (k, i))
        dims = _TN
    else:
        a_spec = pl.BlockSpec((tm, tk), lambda i, j, k: (i, k))
        dims = _NN
    in_specs = [a_spec, pl.BlockSpec((tk, tn), lambda i, j, k: (k, j))]
    args = [a, b]
    if res is not None:
        in_specs.append(pl.BlockSpec((tm, tn), lambda i, j, k: (i, j)))
        args.append(res)
    assert nk == 1 or out_dtype == F32
    return pl.pallas_call(
        functools.partial(_mm_body, nk=nk, dims=dims, has_res=res is not None),
        grid=(m // tm, n // tn, nk),
        in_specs=in_specs,
        out_specs=pl.BlockSpec((tm, tn), lambda i, j, k: (i, j)),
        out_shape=jax.ShapeDtypeStruct((m, n), out_dtype),
        compiler_params=_params("parallel", "parallel", "arbitrary"),
        name="matmul",
    )(*args)


def _mm2_body(a1_ref, a2_ref, b_ref, r_ref, o_ref):
    k1 = a1_ref.shape[1]
    p = lax.dot_general(a1_ref[...], b_ref[0:k1, :], _NN, preferred_element_type=F32)
    p = p + lax.dot_general(a2_ref[...], b_ref[k1:, :], _NN, preferred_element_type=F32)
    o_ref[...] = (r_ref[...] + p).astype(o_ref.dtype)


def _matmul_cat(a1, a2, b, res):
    m, k1 = a1.shape
    k2 = a2.shape[1]
    n = b.shape[1]
    tm = _pick(m, (1024, 512, 256, 128, 64))
    tn = _pick(n, (768, 512, 384, 256, 128))
    return pl.pallas_call(
        _mm2_body,
        grid=(m // tm, n // tn),
        in_specs=[pl.BlockSpec((tm, k1), lambda i, j: (i, 0)),
                  pl.BlockSpec((tm, k2), lambda i, j: (i, 0)),
                  pl.BlockSpec((k1 + k2, tn), lambda i, j: (0, j)),
                  pl.BlockSpec((tm, tn), lambda i, j: (i, j))],
        out_specs=pl.BlockSpec((tm, tn), lambda i, j: (i, j)),
        out_shape=jax.ShapeDtypeStruct((m, n), F32),
        compiler_params=_params("parallel", "parallel"),
        name="matmul_cat",
    )(a1, a2, b, res)


def _head_body(*refs, nh, has_norm, has_rope, scale, n_out):
    it = iter(refs)
    p_ref = next(it)
    g_ref = next(it) if has_norm else None
    if has_rope:
        c = next(it)[...]
        sa = next(it)[...]
        sb = next(it)[...]
    outs = [next(it) for _ in range(n_out)]
    for h in range(nh):
        sl = slice(h * HEAD_DIM, (h + 1) * HEAD_DIM)
        x = p_ref[:, sl]
        if has_norm:
            ms = jnp.mean(x * x, axis=-1, keepdims=True)
            x = x * lax.rsqrt(ms + EPS) * g_ref[...]
        if has_rope:
            x = x * c + pltpu.roll(x, LANES - ROPE_DIM // 2, 1) * sa + pltpu.roll(x, ROPE_DIM // 2, 1) * sb
        if scale != 1.0:
            x = x * scale
        for o in outs:
            o[:, sl] = x.astype(o.dtype)


def _head_post(p, col0, nh, t_len, out_dtypes, gain=None, rope=None, scale=1.0):
    n = p.shape[0]
    w = nh * HEAD_DIM
    assert col0 % w == 0
    tr = _pick(t_len, (256, 128, 64))
    tpb = t_len // tr
    in_specs = [pl.BlockSpec((tr, w), lambda i: (i, col0 // w))]
    args = [p]
    if gain is not None:
        in_specs.append(pl.BlockSpec((1, HEAD_DIM), lambda i: (0, 0)))
        args.append(gain.reshape(1, HEAD_DIM))
    if rope is not None:
        for tab in rope:
            in_specs.append(pl.BlockSpec((tr, LANES), lambda i: (i % tpb, 0)))
            args.append(tab)
    outs = pl.pallas_call(
        functools.partial(_head_body, nh=nh, has_norm=gain is not None, has_rope=rope is not None,
                          scale=scale, n_out=len(out_dtypes)),
        grid=(n // tr,),
        in_specs=in_specs,
        out_specs=[pl.BlockSpec((tr, w), lambda i: (i, 0)) for _ in out_dtypes],
        out_shape=[jax.ShapeDtypeStruct((n, w), dt) for dt in out_dtypes],
        compiler_params=_params("parallel"),
        name="head_post",
    )(*args)
    return outs


def _rope_tables(pos):
    half = ROPE_DIM // 2
    inv_freq = jnp.power(ROPE_THETA, -jnp.arange(half, dtype=F32) / half)
    ang = pos.astype(F32)[:, None] * inv_freq[None, :]
    cos, sin = jnp.cos(ang), jnp.sin(ang)
    t = pos.shape[0]
    zh = jnp.zeros((t, half), F32)
    zr = jnp.zeros((t, HEAD_DIM - ROPE_DIM), F32)
    c = jnp.concatenate([cos, cos, jnp.ones((t, HEAD_DIM - ROPE_DIM), F32)], axis=1)
    sa = jnp.concatenate([-sin, zh, zr], axis=1)
    sb = jnp.concatenate([zh, sin, zr], axis=1)
    return c, sa, sb


def _norm_heads(x, gain):
    ms = jnp.mean(x * x, axis=-1, keepdims=True)
    return x * lax.rsqrt(ms + EPS) * gain


def _rope_heads(x, c, sa, sb):
    return x * c + pltpu.roll(x, LANES - ROPE_DIM // 2, 1) * sa + pltpu.roll(x, ROPE_DIM // 2, 1) * sb


def _proj_heads_body(a_ref, b_ref, g_ref, c_ref, sa_ref, sb_ref, o_ref, *, has_norm, scale):
    p = lax.dot_general(a_ref[...], b_ref[...], _NN, preferred_element_type=F32)
    c, sa, sb = c_ref[...], sa_ref[...], sb_ref[...]
    for h in range(p.shape[1] // HEAD_DIM):
        sl = slice(h * HEAD_DIM, (h + 1) * HEAD_DIM)
        x = p[:, sl]
        if has_norm:
            x = _norm_heads(x, g_ref[...])
        x = _rope_heads(x, c, sa, sb)
        if scale != 1.0:
            x = x * scale
        o_ref[:, sl] = x.astype(o_ref.dtype)


def _proj_heads(a, b, col0, width, rope, tm, gain=None, scale=1.0):
    m, kdim = a.shape
    tn = _pick(math.gcd(width, col0) if col0 else width, (1024, 512, 256, 128))
    assert col0 % tn == 0 and width % tn == 0
    tpb = rope[0].shape[0] // tm
    g = jnp.ones((HEAD_DIM,), F32) if gain is None else gain
    tab_spec = pl.BlockSpec((tm, LANES), lambda i, j: (i % tpb, 0))
    return pl.pallas_call(
        functools.partial(_proj_heads_body, has_norm=gain is not None, scale=scale),
        grid=(m // tm, width // tn),
        in_specs=[pl.BlockSpec((tm, kdim), lambda i, j: (i, 0)),
                  pl.BlockSpec((kdim, tn), lambda i, j: (0, col0 // tn + j)),
                  pl.BlockSpec((1, HEAD_DIM), lambda i, j: (0, 0)),
                  tab_spec, tab_spec, tab_spec],
        out_specs=pl.BlockSpec((tm, tn), lambda i, j: (i, j)),
        out_shape=jax.ShapeDtypeStruct((m, width), BF16),
        compiler_params=_params("parallel", "parallel"),
        name="proj_heads",
    )(a, b, g.reshape(1, HEAD_DIM), *rope)


def _proj_glu_body(a_ref, ba_ref, bg_ref, o_ref):
    x = a_ref[...]
    pa = lax.dot_general(x, ba_ref[...], _NN, preferred_element_type=F32)
    pg = lax.dot_general(x, bg_ref[...], _NN, preferred_element_type=F32)
    o_ref[...] = pa * jax.nn.sigmoid(pg)


def _proj_glu(a, b, col_a, col_g, width, tm):
    m, kdim = a.shape
    tn = _pick(math.gcd(width, math.gcd(col_a, col_g)) if col_a or col_g else width, (512, 256, 128))
    assert col_a % tn == 0 and col_g % tn == 0 and width % tn == 0
    return pl.pallas_call(
        _proj_glu_body,
        grid=(m // tm, width // tn),
        in_specs=[pl.BlockSpec((tm, kdim), lambda i, j: (i, 0)),
                  pl.BlockSpec((kdim, tn), lambda i, j: (0, col_a // tn + j)),
                  pl.BlockSpec((kdim, tn), lambda i, j: (0, col_g // tn + j))],
        out_specs=pl.BlockSpec((tm, tn), lambda i, j: (i, j)),
        out_shape=jax.ShapeDtypeStruct((m, width), F32),
        compiler_params=_params("parallel", "parallel"),
        name="proj_glu",
    )(a, b, b)


KV_W = N_KV_HEADS * HEAD_DIM
TAIL_W = 2 * KV_W + IDX_DIM + LANES


def _proj_tail_body(a_ref, b_ref, g_ref, c_ref, sa_ref, sb_ref, k32_ref, k16_ref, v32_ref, v16_ref,
                    ki32_ref, ki16_ref, wi_ref):
    p = lax.dot_general(a_ref[...], b_ref[...], _NN, preferred_element_type=F32)
    c, sa, sb = c_ref[...], sa_ref[...], sb_ref[...]
    for h in range(N_KV_HEADS):
        sl = slice(h * HEAD_DIM, (h + 1) * HEAD_DIM)
        x = _rope_heads(_norm_heads(p[:, sl], g_ref[...]), c, sa, sb)
        k32_ref[:, h, :] = x
        k16_ref[:, sl] = x.astype(k16_ref.dtype)
    v = p[:, KV_W:2 * KV_W]
    for h in range(N_KV_HEADS):
        v32_ref[:, h, :] = v[:, h * HEAD_DIM:(h + 1) * HEAD_DIM]
    v16_ref[...] = v.astype(v16_ref.dtype)
    ki = _rope_heads(p[:, 2 * KV_W:2 * KV_W + IDX_DIM], c, sa, sb)
    ki32_ref[...] = ki
    ki16_ref[...] = ki.astype(ki16_ref.dtype)
    wi_ref[...] = p[:, 2 * KV_W + IDX_DIM:]


def _proj_tail(a, b, rope, tm, k_gain):
    m, kdim = a.shape
    assert b.shape[1] == TAIL_W
    tpb = rope[0].shape[0] // tm
    tab_spec = pl.BlockSpec((tm, LANES), lambda i: (i % tpb, 0))
    heads = (N_KV_HEADS, HEAD_DIM)
    shapes = (heads, (KV_W,), heads, (KV_W,), (IDX_DIM,), (IDX_DIM,), (LANES,))
    dtypes = (F32, BF16, F32, BF16, F32, BF16, F32)
    return pl.pallas_call(
        _proj_tail_body,
        grid=(m // tm,),
        in_specs=[pl.BlockSpec((tm, kdim), lambda i: (i, 0)),
                  pl.BlockSpec((kdim, TAIL_W), lambda i: (0, 0)),
                  pl.BlockSpec((1, HEAD_DIM), lambda i: (0, 0)),
                  tab_spec, tab_spec, tab_spec],
        out_specs=[pl.BlockSpec((tm,) + sh, lambda i, nd=len(sh): (i,) + (0,) * nd) for sh in shapes],
        out_shape=[jax.ShapeDtypeStruct((m,) + sh, dt) for sh, dt in zip(shapes, dtypes)],
        compiler_params=_params("parallel"),
        name="proj_tail",
    )(a, b, k_gain.reshape(1, HEAD_DIM), *rope)


CONV_HALO = 32
CONV_CW = 256
CONV_TB = 64
SUBLANES = 8


def _conv_body(u_ref, prev_ref, w_ref, b_ref, g_ref, beta_ref, o_ref, win_ref, rot_ref, c_ref, *, tt, ch):
    @pl.when(pl.program_id(1) == 0)
    def _():
        win_ref[0:CONV_HALO, :] = prev_ref[0]

    win_ref[CONV_HALO:CONV_HALO + tt, :] = u_ref[0]
    base = CONV_HALO - CONV_STATE
    for cc in range(ch // CONV_CW):
        cs = slice(cc * CONV_CW, (cc + 1) * CONV_CW)
        for r in range(SUBLANES):
            rows = SUBLANES * ((CONV_WIDTH - 1 - r) // SUBLANES) + tt
            rot_ref[r, 0:rows, :] = win_ref[base + r:base + r + rows, cs]
        for tb in range(tt // CONV_TB):
            acc = None
            for j in range(CONV_WIDTH):
                q, r = divmod(j, SUBLANES)
                lo = SUBLANES * q + tb * CONV_TB
                term = rot_ref[r, lo:lo + CONV_TB, :] * w_ref[j:j + 1, cs]
                acc = term if acc is None else acc + term
            c_ref[tb * CONV_TB:(tb + 1) * CONV_TB, cs] = acc + b_ref[:, cs]
    c = c_ref[...]
    mu = jnp.mean(c, axis=-1, keepdims=True)
    d = c - mu
    var = jnp.mean(d * d, axis=-1, keepdims=True)
    y = d * lax.rsqrt(var + EPS) * g_ref[...] + beta_ref[...]
    o_ref[0] = (y * jax.nn.sigmoid(y)).astype(o_ref.dtype)
    win_ref[0:CONV_HALO, :] = win_ref[tt:tt + CONV_HALO, :]


def _conv_module(u, prev, dw_w, dw_b, ln_g, ln_b):
    b, t, ch = u.shape
    tt = _pick(t, (256, 128, 64))
    prev_p = jnp.pad(prev, ((0, 0), (CONV_HALO - CONV_STATE, 0), (0, 0)))
    w_p = jnp.pad(dw_w, ((0, CONV_HALO - CONV_WIDTH), (0, 0)))
    row = lambda v: v.reshape(1, ch)
    rot_rows = SUBLANES * ((CONV_WIDTH - 1) // SUBLANES) + tt
    return pl.pallas_call(
        functools.partial(_conv_body, tt=tt, ch=ch),
        grid=(b, t // tt),
        in_specs=[pl.BlockSpec((1, tt, ch), lambda i, j: (i, j, 0)),
                  pl.BlockSpec((1, CONV_HALO, ch), lambda i, j: (i, 0, 0)),
                  pl.BlockSpec((CONV_HALO, ch), lambda i, j: (0, 0)),
                  pl.BlockSpec((1, ch), lambda i, j: (0, 0)),
                  pl.BlockSpec((1, ch), lambda i, j: (0, 0)),
                  pl.BlockSpec((1, ch), lambda i, j: (0, 0))],
        out_specs=pl.BlockSpec((1, tt, ch), lambda i, j: (i, j, 0)),
        out_shape=jax.ShapeDtypeStruct((b, t, ch), BF16),
        scratch_shapes=[pltpu.VMEM((tt + CONV_HALO, ch), F32),
                        pltpu.VMEM((SUBLANES, rot_rows, CONV_CW), F32),
                        pltpu.VMEM((tt, ch), F32)],
        compiler_params=_params("arbitrary", "arbitrary"),
        name="conv_module",
    )(u, prev_p, w_p, row(dw_b), row(ln_g), row(ln_b))


def _append_body(ck_ref, cv_ref, ci_ref, nk_ref, nv_ref, ni_ref, k_ref, v_ref, i_ref, *, n_past):
    c = pl.program_id(1)

    @pl.when(c < n_past)
    def _():
        for g in range(N_KV_HEADS):
            sl = slice(g * HEAD_DIM, (g + 1) * HEAD_DIM)
            k_ref[0, :, sl] = ck_ref[0, :, g, :].astype(k_ref.dtype)
            v_ref[0, :, sl] = cv_ref[0, :, g, :].astype(v_ref.dtype)
        i_ref[0] = ci_ref[0].astype(i_ref.dtype)

    @pl.when(c >= n_past)
    def _():
        k_ref[0] = nk_ref[0]
        v_ref[0] = nv_ref[0]
        i_ref[0] = ni_ref[0]


def _append_cache(k_past, v_past, ki_past, k_new, v_new, ki_new):
    b, past = k_past.shape[0], k_past.shape[1]
    sc = DSA_SC
    assert past % sc == 0 and k_new.shape[1] <= sc
    n_past = past // sc
    padw = ((0, 0), (0, sc - k_new.shape[1]), (0, 0))
    k_new, v_new, ki_new = jnp.pad(k_new, padw), jnp.pad(v_new, padw), jnp.pad(ki_new, padw)
    last = n_past - 1
    past4 = pl.BlockSpec((1, sc, N_KV_HEADS, HEAD_DIM), lambda i, c: (i, jnp.minimum(c, last), 0, 0))
    new3 = lambda w: pl.BlockSpec((1, sc, w), lambda i, c: (i, 0, 0))
    out3 = lambda w: pl.BlockSpec((1, sc, w), lambda i, c: (i, c, 0))
    s_pad = past + sc
    return pl.pallas_call(
        functools.partial(_append_body, n_past=n_past),
        grid=(b, n_past + 1),
        in_specs=[past4, past4, pl.BlockSpec((1, sc, IDX_DIM), lambda i, c: (i, jnp.minimum(c, last), 0)),
                  new3(KV_W), new3(KV_W), new3(IDX_DIM)],
        out_specs=[out3(KV_W), out3(KV_W), out3(IDX_DIM)],
        out_shape=[jax.ShapeDtypeStruct((b, s_pad, KV_W), BF16), jax.ShapeDtypeStruct((b, s_pad, KV_W), BF16),
                   jax.ShapeDtypeStruct((b, s_pad, IDX_DIM), BF16)],
        compiler_params=_params("parallel", "arbitrary"),
        name="append_cache",
    )(k_past, v_past, ki_past, k_new, v_new, ki_new)


DSA_SC = 512


def _dsa_body(qs_ref, k_ref, v_ref, qi_ref, ki_ref, wi_ref, o_ref, key_ref, bias_ref, thr_ref, jcut_ref,
              qst_ref, mx_ref, acc_ref, *, tq, nchunks, buckets, s_valid, topk, pos0, n_heads):
    sc = DSA_SC
    group = n_heads // N_KV_HEADS
    rows = group * tq
    q_lo = pos0 + pl.program_id(1) * tq
    qpos = q_lo + lax.broadcasted_iota(jnp.int32, (tq, 1), 0)
    chunk_shift = CHUNK.bit_length() - 1
    qchunk = qpos >> chunk_shift
    k_end = jnp.minimum((((q_lo + tq - 1) >> chunk_shift) + 1) << chunk_shift, s_valid)
    nact = jnp.minimum((k_end + sc - 1) >> (sc.bit_length() - 1), nchunks)
    nbucket = jnp.int32(buckets[-1])
    for bsz in reversed(buckets[:-1]):
        nbucket = jnp.where(nact <= bsz, bsz, nbucket)

    def score_chunk(c, carry):
        off = pl.multiple_of(c * sc, sc)
        ki_c = ki_ref[0, pl.ds(off, sc), :]
        acc = jnp.zeros((tq, sc), F32)
        for h in range(IDX_HEADS):
            s = lax.dot_general(qi_ref[0, :, h * IDX_DIM:(h + 1) * IDX_DIM], ki_c, _NT,
                                preferred_element_type=F32)
            acc = acc + wi_ref[0, :, h:h + 1] * jnp.maximum(s, 0.0)
        score = acc * IDX_SCALE
        spos = off + lax.broadcasted_iota(jnp.int32, (1, sc), 1)
        schunk = spos >> chunk_shift
        adm = jnp.logical_and(schunk <= qchunk, spos < s_valid)
        bits = pltpu.bitcast(score, jnp.int32)
        key = bits ^ ((bits >> 31) & 0x7FFFFFFF)
        key_ref[c] = jnp.where(adm, key, INT_MIN)
        return carry

    lax.fori_loop(0, nact, score_chunk, 0)

    def fill_chunk(c, carry):
        key_ref[c] = jnp.full((tq, sc), INT_MIN, jnp.int32)
        return carry

    lax.fori_loop(nact, nbucket, fill_chunk, 0)

    for bsz in buckets:
        @pl.when(nbucket == bsz)
        def _(bsz=bsz):
            def count(mask):
                return jnp.sum(jnp.sum(mask.astype(F32), axis=0), axis=-1, keepdims=True)

            def count_ge(cand):
                return count(key_ref[0:bsz] >= cand[None])

            zero = jnp.zeros((tq, 1), jnp.int32)
            ans = jnp.where(count_ge(zero) >= topk, zero, INT_MIN)

            def bit_step(i, ans):
                cand = ans | lax.shift_left(jnp.int32(1), 30 - i)
                return jnp.where(count_ge(cand) >= topk, cand, ans)

            ans = lax.fori_loop(0, 31, bit_step, ans)
            thr = jnp.maximum(ans, INT_MIN + 1)
            thr_ref[...] = thr
            need = topk - count(key_ref[0:bsz] > thr[None])
            ties = count(key_ref[0:bsz] == thr[None])
            nbits = (bsz * sc).bit_length()
            jcut_ref[...] = jnp.full((tq, 1), (1 << nbits) - 1, jnp.int32)

            @pl.when(jnp.max(ties - need) > 0.0)
            def _():
                pos = (lax.broadcasted_iota(jnp.int32, (bsz, tq, sc), 0) * sc
                       + lax.broadcasted_iota(jnp.int32, (bsz, tq, sc), 2))

                def ties_before(j):
                    keys = key_ref[0:bsz]
                    return count(jnp.where(keys == thr_ref[...][None], pos, (1 << nbits)) < j[None])

                def pos_step(i, j):
                    cand = j | lax.shift_left(jnp.int32(1), nbits - 1 - i)
                    return jnp.where(ties_before(cand) <= need, cand, j)

                jcut_ref[...] = lax.fori_loop(0, nbits, pos_step, zero)

    thr = thr_ref[...]
    jcut = jcut_ref[...]

    def bias_chunk(c, carry):
        spos = c * sc + lax.broadcasted_iota(jnp.int32, (1, sc), 1)
        bound = thr - jnp.where(spos < jcut, 1, 0)
        bias_ref[c] = jnp.where(key_ref[c] > bound, 0.0, NEG_BIG)
        return carry

    lax.fori_loop(0, nact, bias_chunk, 0)

    for hd in range(n_heads):
        qst_ref[hd * tq:(hd + 1) * tq, :] = qs_ref[0, :, hd * HEAD_DIM:(hd + 1) * HEAD_DIM]
    ntile = sc // LANES

    def masked_logits(c, g):
        off = pl.multiple_of(c * sc, sc)
        k_c = k_ref[0, pl.ds(off, sc), g * HEAD_DIM:(g + 1) * HEAD_DIM]
        lg = lax.dot_general(qst_ref[g * rows:(g + 1) * rows, :], k_c, _NT, preferred_element_type=F32)
        return (lg.reshape(group, tq, sc) + bias_ref[c][None]).reshape(rows, sc)

    mx_ref[...] = jnp.full(mx_ref.shape, NEG_BIG, F32)

    def max_chunk(c, carry):
        for g in range(N_KV_HEADS):
            rs = slice(g * rows, (g + 1) * rows)
            lg = masked_logits(c, g)
            mx = mx_ref[rs, :]
            for t in range(ntile):
                mx = jnp.maximum(mx, lg[:, t * LANES:(t + 1) * LANES])
            mx_ref[rs, :] = mx
        return carry

    lax.fori_loop(0, nact, max_chunk, 0)
    mx_ref[...] = jnp.broadcast_to(jnp.max(mx_ref[...], axis=-1, keepdims=True), mx_ref.shape)

    acc_ref[...] = jnp.zeros(acc_ref.shape, F32)
    ones = jnp.ones((sc, HEAD_DIM), BF16)

    def attn_chunk(c, carry):
        off = pl.multiple_of(c * sc, sc)
        for g in range(N_KV_HEADS):
            rs = slice(g * rows, (g + 1) * rows)
            lg = masked_logits(c, g)
            mb = mx_ref[rs, :]
            p = jnp.concatenate([jnp.exp(lg[:, t * LANES:(t + 1) * LANES] - mb) for t in range(ntile)], axis=1)
            v_ext = jnp.concatenate([v_ref[0, pl.ds(off, sc), g * HEAD_DIM:(g + 1) * HEAD_DIM], ones], axis=1)
            acc_ref[rs, :] += lax.dot_general(p.astype(BF16), v_ext, _NN, preferred_element_type=F32)
        return carry

    lax.fori_loop(0, nact, attn_chunk, 0)
    for hd in range(n_heads):
        hs = slice(hd * tq, (hd + 1) * tq)
        o = acc_ref[hs, 0:HEAD_DIM] / acc_ref[hs, HEAD_DIM:2 * HEAD_DIM]
        o_ref[0, :, hd * HEAD_DIM:(hd + 1) * HEAD_DIM] = o.astype(o_ref.dtype)


def _dsa_attention(qs, k_all, v_all, qi, ki_all, p3, wi_col, s_valid, topk, pos0):
    b, t, qw = qs.shape
    s_pad = k_all.shape[1]
    n_heads = qw // HEAD_DIM
    tq = _pick(t, (128, 64))
    nchunks = s_pad // DSA_SC
    if pos0 + tq >= s_valid:
        buckets = (nchunks,)
    else:
        buckets = tuple(sorted({min(1 << e, nchunks) for e in range(nchunks.bit_length() + 1)}))
    return pl.pallas_call(
        functools.partial(_dsa_body, tq=tq, nchunks=nchunks, buckets=buckets, s_valid=s_valid, topk=topk,
                          pos0=pos0, n_heads=n_heads),
        grid=(b, t // tq),
        in_specs=[pl.BlockSpec((1, tq, qw), lambda i, j: (i, j, 0)),
                  pl.BlockSpec((1, s_pad, N_KV_HEADS * HEAD_DIM), lambda i, j: (i, 0, 0)),
                  pl.BlockSpec((1, s_pad, N_KV_HEADS * HEAD_DIM), lambda i, j: (i, 0, 0)),
                  pl.BlockSpec((1, tq, IDX_HEADS * IDX_DIM), lambda i, j: (i, j, 0)),
                  pl.BlockSpec((1, s_pad, IDX_DIM), lambda i, j: (i, 0, 0)),
                  pl.BlockSpec((1, tq, LANES), lambda i, j: (i, j, wi_col // LANES))],
        out_specs=pl.BlockSpec((1, tq, qw), lambda i, j: (i, j, 0)),
        out_shape=jax.ShapeDtypeStruct((b, t, qw), BF16),
        scratch_shapes=[pltpu.VMEM((nchunks, tq, DSA_SC), jnp.int32),
                        pltpu.VMEM((nchunks, tq, DSA_SC), F32),
                        pltpu.VMEM((tq, 1), jnp.int32),
                        pltpu.VMEM((tq, 1), jnp.int32),
                        pltpu.VMEM((n_heads * tq, HEAD_DIM), BF16),
                        pltpu.VMEM((n_heads * tq, LANES), F32),
                        pltpu.VMEM((n_heads * tq, 2 * HEAD_DIM), F32)],
        compiler_params=_params("parallel", "arbitrary"),
        name="dsa_attention",
    )(qs, k_all, v_all, qi, ki_all, p3)


def _memattn_body(q_ref, g_ref, mk_ref, mv_ref, o_ref):
    for h in range(MEM_HEADS):
        sl = slice(h * HEAD_DIM, (h + 1) * HEAD_DIM)
        x = q_ref[0, :, sl]
        ms = jnp.mean(x * x, axis=-1, keepdims=True)
        qn = (x * lax.rsqrt(ms + EPS) * g_ref[...] * ATTN_SCALE).astype(BF16)
        lg = lax.dot_general(qn, mk_ref[0, :, sl], _NT, preferred_element_type=F32)
        m = jnp.max(lg, axis=-1, keepdims=True)
        p = jnp.exp(lg - m)
        l = jnp.sum(p, axis=-1, keepdims=True)
        o = lax.dot_general(p.astype(BF16), mv_ref[0, :, sl], _NN, preferred_element_type=F32)
        o_ref[0, :, sl] = (o / l).astype(o_ref.dtype)


def _mem_attention(qm, gain, mk, mv):
    b, t, w = qm.shape
    m = mk.shape[1]
    tq = _pick(t, (512, 256, 128, 64))
    return pl.pallas_call(
        _memattn_body,
        grid=(b, t // tq),
        in_specs=[pl.BlockSpec((1, tq, w), lambda i, j: (i, j, 0)),
                  pl.BlockSpec((1, HEAD_DIM), lambda i, j: (0, 0)),
                  pl.BlockSpec((1, m, w), lambda i, j: (i, 0, 0)),
                  pl.BlockSpec((1, m, w), lambda i, j: (i, 0, 0))],
        out_specs=pl.BlockSpec((1, tq, w), lambda i, j: (i, j, 0)),
        out_shape=jax.ShapeDtypeStruct((b, t, w), BF16),
        compiler_params=_params("parallel", "parallel"),
        name="mem_attention",
    )(qm, gain.reshape(1, HEAD_DIM), mk, mv)


PEER_HALF = PEER_QDIM // 2
_PEER_PAIRS = [(a, b) for a in range(PEER_TOPK) for b in range(PEER_TOPK) if (a + 1) * (b + 1) <= PEER_TOPK]


def _route_body(pq_ref, k1_ref, k2_ref, r2_ref, e2_ref, na_ref, e1_ref, rk1_ref, v1_ref, v2_ref, c_ref, *, tt):
    row = lax.broadcasted_iota(jnp.int32, (PEER_KEYS, tt), 0)
    for h in range(PEER_HEADS):
        for table, (kref, vref, lo) in enumerate(((k1_ref, v1_ref, 0), (k2_ref, v2_ref, PEER_HALF))):
            q = pq_ref[:, h * PEER_QDIM + lo:h * PEER_QDIM + lo + PEER_HALF]
            s = lax.dot_general(kref[h], q, _NT, preferred_element_type=F32)
            cur = s
            rank = jnp.full((PEER_KEYS, tt), float(PEER_TOPK), F32)
            for r in range(PEER_TOPK):
                m = jnp.max(cur, axis=0, keepdims=True)
                vref[r, h:h + 1, :] = m
                first = jnp.min(jnp.where(cur == m, row, PEER_KEYS), axis=0, keepdims=True)
                hit = row == first
                cur = jnp.where(hit, -jnp.inf, cur)
                rank = jnp.where(hit, float(r), rank)
            ex = jnp.exp(s - vref[0, h:h + 1, :])
            if table == 0:
                rk1_ref[h] = rank
                e1_ref[h] = ex
            else:
                r2_ref[h] = rank.astype(r2_ref.dtype)
                e2_ref[h] = ex.astype(e2_ref.dtype)
    for p, (a, b) in enumerate(_PEER_PAIRS):
        c_ref[p] = v1_ref[a] + v2_ref[b]
    call = c_ref[...]

    def tau_step(p, tau):
        cp = c_ref[p]
        cnt = jnp.sum((call >= cp[None]).astype(F32), axis=0)
        return jnp.maximum(tau, jnp.where(cnt >= PEER_TOPK, cp, -jnp.inf))

    tau = lax.fori_loop(0, len(_PEER_PAIRS), tau_step, jnp.full((PEER_HEADS, tt), -jnp.inf, F32))
    zero = jnp.zeros((PEER_HEADS, tt), F32)
    cnt_gt = [zero] * PEER_TOPK
    cnt_eq = [zero] * PEER_TOPK
    for p, (a, b) in enumerate(_PEER_PAIRS):
        cnt_gt[a] = cnt_gt[a] + (call[p] > tau).astype(F32)
        cnt_eq[a] = cnt_eq[a] + (call[p] == tau).astype(F32)
    rem = float(PEER_TOPK) - sum(cnt_gt)
    n_of = []
    for a in range(PEER_TOPK):
        take = jnp.minimum(cnt_eq[a], rem)
        rem = rem - take
        n_of.append(cnt_gt[a] + take)
    z = zero
    for p, (a, b) in enumerate(_PEER_PAIRS):
        z = z + jnp.where(n_of[a] > float(b), jnp.exp(call[p] - call[0]), 0.0)
    inv_z = 1.0 / z
    for h in range(PEER_HEADS):
        rk = rk1_ref[h]
        na = jnp.zeros((PEER_KEYS, tt), F32)
        for a in range(PEER_TOPK):
            na = jnp.where(rk == float(a), n_of[a][h:h + 1, :], na)
        na_ref[h] = na
        e1_ref[h] = e1_ref[h] * inv_z[h:h + 1, :]


def _peer_route(pq, k1, k2):
    n = pq.shape[0]
    tt = _pick(n, (256, 128))
    shape = (PEER_HEADS, PEER_KEYS, n)
    big_spec = pl.BlockSpec((PEER_HEADS, PEER_KEYS, tt), lambda i: (0, 0, i))
    kspec = pl.BlockSpec((PEER_HEADS, PEER_KEYS, PEER_HALF), lambda i: (0, 0, 0))
    return pl.pallas_call(
        functools.partial(_route_body, tt=tt),
        grid=(n // tt,),
        in_specs=[pl.BlockSpec((tt, PEER_HEADS * PEER_QDIM), lambda i: (i, 0)), kspec, kspec],
        out_specs=[big_spec, big_spec, big_spec, big_spec],
        out_shape=[jax.ShapeDtypeStruct(shape, BF16), jax.ShapeDtypeStruct(shape, BF16),
                   jax.ShapeDtypeStruct(shape, F32), jax.ShapeDtypeStruct(shape, F32)],
        scratch_shapes=[pltpu.VMEM((PEER_HEADS, PEER_KEYS, tt), F32),
                        pltpu.VMEM((PEER_TOPK, PEER_HEADS, tt), F32),
                        pltpu.VMEM((PEER_TOPK, PEER_HEADS, tt), F32),
                        pltpu.VMEM((len(_PEER_PAIRS), PEER_HEADS, tt), F32)],
        compiler_params=_params("parallel"),
        name="peer_route",
    )(pq, k1, k2)


PEER_EB = 1024


def _coef_body(u_ref, x_ref, r2_ref, e2_ref, na_ref, e1_ref, o_ref, *, tt):
    nb = PEER_EB // PEER_KEYS
    e = pl.program_id(1)
    act = lax.dot_general(u_ref[...], x_ref[...], _NT, preferred_element_type=F32)
    gdt = r2_ref.dtype
    for j in range(nb):
        i1 = e * nb + j
        gate = jnp.zeros((PEER_KEYS, tt), gdt)
        for h in range(PEER_HEADS):
            na = jnp.broadcast_to(na_ref[h, pl.ds(i1, 1), :].astype(gdt), (PEER_KEYS, tt))
            e1 = jnp.broadcast_to(e1_ref[h, pl.ds(i1, 1), :].astype(gdt), (PEER_KEYS, tt))
            gate = gate + jnp.where(r2_ref[h] < na, e1 * e2_ref[h], jnp.zeros((), gdt))
        a = act[j * PEER_KEYS:(j + 1) * PEER_KEYS, :]
        gelu = 0.5 * a * (1.0 + lax.erf(a * math.sqrt(0.5)))
        o_ref[j * PEER_KEYS:(j + 1) * PEER_KEYS, :] = (gate * gelu.astype(gdt)).astype(o_ref.dtype)


def _peer_coef(u_tab, xn, r2, e2, na, e1):
    ne, d = u_tab.shape
    n = xn.shape[0]
    tt = _pick(n, (512, 256, 128))
    big_spec = pl.BlockSpec((PEER_HEADS, PEER_KEYS, tt), lambda i, e: (0, 0, i))
    return pl.pallas_call(
        functools.partial(_coef_body, tt=tt),
        grid=(n // tt, ne // PEER_EB),
        in_specs=[pl.BlockSpec((PEER_EB, d), lambda i, e: (e, 0)),
                  pl.BlockSpec((tt, d), lambda i, e: (i, 0)),
                  big_spec, big_spec, big_spec, big_spec],
        out_specs=pl.BlockSpec((PEER_EB, tt), lambda i, e: (e, i)),
        out_shape=jax.ShapeDtypeStruct((ne, n), BF16),
        compiler_params=_params("parallel", "arbitrary"),
        name="peer_coef",
    )(u_tab, xn, r2, e2, na, e1)


def _round_up(x, m):
    return -(-x // m) * m


def _layer(x, pos0, conv_prev, k_past, v_past, ki_past, mem_k, mem_v, w):
    b, t, d = x.shape
    n = b * t
    ch = d // 2
    n_heads = (d - ch) // HEAD_DIM
    off = w["offsets"]
    pos = pos0 + jnp.arange(t, dtype=jnp.int32)
    rope = _rope_tables(pos)

    x2 = x.reshape(n, d)
    hn = _rmsnorm(x2, w["norm_mix_g"])
    tm = _pick(n, (1024, 512, 256, 128, 64))
    if tm > t:
        rope = tuple(jnp.tile(tab, (tm // t, 1)) for tab in rope)
    u = _proj_glu(hn, w["w_in"], off["a"], off["g"], ch, tm)
    qs = _proj_heads(hn, w["w_in"], off["q"], n_heads * HEAD_DIM, rope, tm, gain=w["q_norm_g"], scale=ATTN_SCALE)
    qi = _proj_heads(hn, w["w_in"], off["qi"], IDX_HEADS * IDX_DIM, rope, tm)
    k32, k16, v32, v16, ki32, ki16, wi = _proj_tail(hn, w["w_tail"], rope, tm, w["k_norm_g"])
    kvw = KV_W

    u3 = u.reshape(b, t, ch)
    prev = jnp.zeros((b, CONV_STATE, ch), F32) if conv_prev is None else conv_prev
    conv_out = _conv_module(u3, prev, w["dw_w"], w["dw_b"], w["conv_ln_g"], w["conv_ln_b"])
    conv_new = u3[:, t - CONV_STATE:, :]

    k3, v3, ki3 = k16.reshape(b, t, kvw), v16.reshape(b, t, kvw), ki16.reshape(b, t, IDX_DIM)
    s_valid = t
    if k_past is not None:
        s_valid = k_past.shape[1] + t
        k3, v3, ki3 = _append_cache(k_past, v_past, ki_past, k3, v3, ki3)
    s_pad = _round_up(k3.shape[1], DSA_SC)
    if s_pad != k3.shape[1]:
        padw = ((0, 0), (0, s_pad - k3.shape[1]), (0, 0))
        k3, v3, ki3 = jnp.pad(k3, padw), jnp.pad(v3, padw), jnp.pad(ki3, padw)
    topk = min(TOPK_MAX, s_valid // 4)
    attn = _dsa_attention(qs.reshape(b, t, n_heads * HEAD_DIM), k3, v3, qi.reshape(b, t, IDX_HEADS * IDX_DIM),
                          ki3, wi.reshape(b, t, LANES), 0, s_valid, topk, pos0)

    h1 = _matmul_cat(conv_out.reshape(n, ch), attn.reshape(n, n_heads * HEAD_DIM), w["w_out"], x2)

    hn2 = _rmsnorm(h1, w["norm_mem_g"])
    qm = _matmul(hn2, w["w_q_mem"])
    om = _mem_attention(qm.reshape(b, t, MEM_DIM), w["mem_q_norm_g"], mem_k, mem_v)
    h2 = _matmul(om.reshape(n, MEM_DIM), w["w_o_mem"], res=h1)

    hn3 = _rmsnorm(h2, w["norm_ffn_g"])
    pq = _matmul(hn3, w["peer_wq"], out_dtype=BF16)
    r2, e2, na, e1 = _peer_route(pq, w["peer_sub_k1"], w["peer_sub_k2"])
    coef_t = _peer_coef(w["peer_u"], hn3, r2, e2, na, e1)
    ne = coef_t.shape[0]
    y = _matmul(coef_t, w["peer_v"], res=h2, trans_a=True,
                tm=_pick(n, (1024, 512)), tn=_pick(d, (1024, 512)), tk=_pick(ne, (2048, 1024)))

    return (y.reshape(b, t, d), k32.reshape(b, t, N_KV_HEADS, HEAD_DIM), v32.reshape(b, t, N_KV_HEADS, HEAD_DIM),
            ki32.reshape(b, t, IDX_DIM), conv_new)


def _mem_kv(mem, mem_norm_g, w_kv, mem_k_norm_g):
    b, m, d = mem.shape
    mn = _rmsnorm(mem.reshape(b * m, d), mem_norm_g)
    mkv = _matmul(mn, w_kv)
    mk32, mk16 = _head_post(mkv, 0, MEM_HEADS, m, (F32, BF16), gain=mem_k_norm_g)
    mv32 = mkv[:, MEM_DIM:]
    return mk32.reshape(b, m, MEM_DIM), mv32.reshape(b, m, MEM_DIM), mk16.reshape(b, m, MEM_DIM)


def _prep_layer_weights(l, d, norm_mix_g, w_in, dw_w, dw_b, conv_ln_g, conv_ln_b, q_norm_g, k_norm_g, w_out,
                        norm_mem_g, w_q_mem, mem_q_norm_g, w_o_mem, norm_ffn_g, peer_wq, peer_sub_k1,
                        peer_sub_k2, peer_u, peer_v):
    ch = d // 2
    nq = d - ch
    niq = IDX_HEADS * IDX_DIM
    offsets = {"a": 0, "g": ch, "q": 2 * ch}
    offsets["k"] = offsets["q"] + nq
    offsets["qi"] = offsets["k"] + 2 * KV_W
    offsets["ki"] = offsets["qi"] + niq
    wl = w_in[l].astype(BF16)
    end = offsets["ki"] + IDX_DIM + IDX_HEADS
    w_tail = jnp.concatenate([wl[:, offsets["k"]:offsets["qi"]], wl[:, offsets["ki"]:end],
                              jnp.zeros((d, LANES - IDX_HEADS), BF16)], axis=1)
    return {
        "offsets": offsets,
        "w_in": wl,
        "w_tail": w_tail,
        "norm_mix_g": norm_mix_g[l], "dw_w": dw_w[l], "dw_b": dw_b[l],
        "conv_ln_g": conv_ln_g[l], "conv_ln_b": conv_ln_b[l],
        "q_norm_g": q_norm_g[l], "k_norm_g": k_norm_g[l],
        "w_out": w_out[l].astype(BF16),
        "norm_mem_g": norm_mem_g[l], "w_q_mem": w_q_mem[l].astype(BF16), "mem_q_norm_g": mem_q_norm_g[l],
        "w_o_mem": w_o_mem[l].astype(BF16),
        "norm_ffn_g": norm_ffn_g[l], "peer_wq": peer_wq[l].astype(BF16),
        "peer_sub_k1": peer_sub_k1[l].astype(BF16), "peer_sub_k2": peer_sub_k2[l].astype(BF16),
        "peer_u": peer_u[l].astype(BF16), "peer_v": peer_v[l].astype(BF16),
    }


def kernel(x_prompt, x_sample, mem_prompt, cache_k, cache_v, cache_k_idx, state_conv, cache_mem_k, cache_mem_v, norm_mix_g, w_in, dw_w, dw_b, conv_ln_g, conv_ln_b, q_norm_g, k_norm_g, w_out, norm_mem_g, mem_norm_g, w_q_mem, w_k_mem, w_v_mem, mem_q_norm_g, mem_k_norm_g, w_o_mem, norm_ffn_g, peer_wq, peer_sub_k1, peer_sub_k2, peer_u, peer_v):
    depth = w_in.shape[0]
    d = x_prompt.shape[-1]
    past_len = cache_k.shape[2]
    h_p, h_s = x_prompt, x_sample
    outs = [[] for _ in range(10)]
    for l in range(depth):
        w = _prep_layer_weights(l, d, norm_mix_g, w_in, dw_w, dw_b, conv_ln_g, conv_ln_b, q_norm_g, k_norm_g,
                                w_out, norm_mem_g, w_q_mem, mem_q_norm_g, w_o_mem, norm_ffn_g, peer_wq,
                                peer_sub_k1, peer_sub_k2, peer_u, peer_v)
        w_kv = jnp.concatenate([w_k_mem[l], w_v_mem[l]], axis=1).astype(BF16)
        mk32, mv32, mk16 = _mem_kv(mem_prompt, mem_norm_g[l], w_kv, mem_k_norm_g[l])
        h_p, kp, vp, kip, cp = _layer(h_p, 0, None, None, None, None, mk16, mv32.astype(BF16), w)
        bs, ms = cache_mem_k.shape[1], cache_mem_k.shape[2]
        h_s, ks, vs, kis, cs = _layer(h_s, past_len, state_conv[l], cache_k[l], cache_v[l], cache_k_idx[l],
                                      cache_mem_k[l].reshape(bs, ms, MEM_DIM).astype(BF16),
                                      cache_mem_v[l].reshape(bs, ms, MEM_DIM).astype(BF16), w)
        bp, mp = mem_prompt.shape[0], mem_prompt.shape[1]
        for lst, val in zip(outs, (kp, vp, kip, cp, mk32.reshape(bp, mp, MEM_HEADS, HEAD_DIM),
                                   mv32.reshape(bp, mp, MEM_HEADS, HEAD_DIM), ks, vs, kis, cs)):
            lst.append(val)
    return (h_p, h_s) + tuple(jnp.stack(lst) for lst in outs)
```

```python
import functools
import math

import jax
import jax.numpy as jnp
from jax import lax
from jax.experimental import pallas as pl
from jax.experimental.pallas import tpu as pltpu

F32 = jnp.float32
BF16 = jnp.bfloat16

CHUNK = 64
CONV_WIDTH = 31
CONV_STATE = CONV_WIDTH - 1
HEAD_DIM = 128
N_KV_HEADS = 4
ROPE_DIM = HEAD_DIM // 4
ROPE_THETA = 500000.0
IDX_HEADS = 32
IDX_DIM = 128
IDX_SCALE = (IDX_HEADS ** -0.5) * (IDX_DIM ** -0.5)
TOPK_MAX = 256
ATTN_SCALE = HEAD_DIM ** -0.5
MEM_HEADS = 4
MEM_DIM = MEM_HEADS * HEAD_DIM
PEER_KEYS = 128
PEER_HEADS = 8
PEER_QDIM = 256
PEER_TOPK = 16
EPS = 1e-6

LANES = 128
VMEM_LIMIT_BYTES = 56 * 1024 * 1024

NEG_BIG = -1e30
INT_MIN = -(2 ** 31)

_NT = (((1,), (1,)), ((), ()))
_NN = (((1,), (0,)), ((), ()))
_TN = (((0,), (0,)), ((), ()))


def _params(*sem):
    return pltpu.CompilerParams(dimension_semantics=sem, vmem_limit_bytes=VMEM_LIMIT_BYTES)


def _pick(n, candidates):
    for c in candidates:
        if c <= n and n % c == 0:
            return c
    return n


def _rmsnorm_body(x_ref, g_ref, o_ref):
    x = x_ref[...]
    ms = jnp.mean(x * x, axis=-1, keepdims=True)
    o_ref[...] = (x * lax.rsqrt(ms + EPS) * g_ref[...]).astype(o_ref.dtype)


def _rmsnorm(x, g):
    n, d = x.shape
    tr = _pick(n, (256, 128, 64))
    return pl.pallas_call(
        _rmsnorm_body,
        grid=(n // tr,),
        in_specs=[pl.BlockSpec((tr, d), lambda i: (i, 0)),
                  pl.BlockSpec((1, d), lambda i: (0, 0))],
        out_specs=pl.BlockSpec((tr, d), lambda i: (i, 0)),
        out_shape=jax.ShapeDtypeStruct((n, d), BF16),
        compiler_params=_params("parallel"),
        name="rmsnorm",
    )(x, g.reshape(1, d))


def _mm_body(*refs, nk, dims, has_res):
    a_ref, b_ref = refs[0], refs[1]
    r_ref = refs[2] if has_res else None
    o_ref = refs[2 + has_res]
    p = lax.dot_general(a_ref[...], b_ref[...], dims, preferred_element_type=F32)
    if nk == 1:
        if has_res:
            p = r_ref[...] + p
        o_ref[...] = p.astype(o_ref.dtype)
        return
    k = pl.program_id(2)

    @pl.when(k == 0)
    def _():
        o_ref[...] = r_ref[...] + p if has_res else p

    @pl.when(k > 0)
    def _():
        o_ref[...] += p


def _matmul(a, b, res=None, out_dtype=F32, trans_a=False, tm=None, tn=None, tk=None):
    if trans_a:
        kdim, m = a.shape
    else:
        m, kdim = a.shape
    n = b.shape[1]
    tm = tm or _pick(m, (1024, 512, 256, 128, 64))
    tn = tn or _pick(n, (768, 512, 384, 256, 128))
    tk = tk or kdim
    nk = kdim // tk
    if trans_a:
        a_spec = pl.BlockSpec((tk, tm), lambda i, j, k: (k, i))
        dims = _TN
    else:
        a_spec = pl.BlockSpec((tm, tk), lambda i, j, k: (i, k))
        dims = _NN
    in_specs = [a_spec, pl.BlockSpec((tk, tn), lambda i, j, k: (k, j))]
    args = [a, b]
    if res is not None:
        in_specs.append(pl.BlockSpec((tm, tn), lambda i, j, k: (i, j)))
        args.append(res)
    assert nk == 1 or out_dtype == F32
    return pl.pallas_call(
        functools.partial(_mm_body, nk=nk, dims=dims, has_res=res is not None),
        grid=(m // tm, n // tn, nk),
        in_specs=in_specs,
        out_specs=pl.BlockSpec((tm, tn), lambda i, j, k: (i, j)),
        out_shape=jax.ShapeDtypeStruct((m, n), out_dtype),
        compiler_params=_params("parallel", "parallel", "arbitrary"),
        name="matmul",
    )(*args)


def _mm2_body(a1_ref, a2_ref, b_ref, r_ref, o_ref):
    k1 = a1_ref.shape[1]
    p = lax.dot_general(a1_ref[...], b_ref[0:k1, :], _NN, preferred_element_type=F32)
    p = p + lax.dot_general(a2_ref[...], b_ref[k1:, :], _NN, preferred_element_type=F32)
    o_ref[...] = (r_ref[...] + p).astype(o_ref.dtype)


def _matmul_cat(a1, a2, b, res):
    m, k1 = a1.shape
    k2 = a2.shape[1]
    n = b.shape[1]
    tm = _pick(m, (1024, 512, 256, 128, 64))
    tn = _pick(n, (768, 512, 384, 256, 128))
    return pl.pallas_call(
        _mm2_body,
        grid=(m // tm, n // tn),
        in_specs=[pl.BlockSpec((tm, k1), lambda i, j: (i, 0)),
                  pl.BlockSpec((tm, k2), lambda i, j: (i, 0)),
                  pl.BlockSpec((k1 + k2, tn), lambda i, j: (0, j)),
                  pl.BlockSpec((tm, tn), lambda i, j: (i, j))],
        out_specs=pl.BlockSpec((tm, tn), lambda i, j: (i, j)),
        out_shape=jax.ShapeDtypeStruct((m, n), F32),
        compiler_params=_params("parallel", "parallel"),
        name="matmul_cat",
    )(a1, a2, b, res)


def _head_body(*refs, nh, has_norm, has_rope, scale, n_out):
    it = iter(refs)
    p_ref = next(it)
    g_ref = next(it) if has_norm else None
    if has_rope:
        c = next(it)[...]
        sa = next(it)[...]
        sb = next(it)[...]
    outs = [next(it) for _ in range(n_out)]
    for h in range(nh):
        sl = slice(h * HEAD_DIM, (h + 1) * HEAD_DIM)
        x = p_ref[:, sl]
        if has_norm:
            ms = jnp.mean(x * x, axis=-1, keepdims=True)
            x = x * lax.rsqrt(ms + EPS) * g_ref[...]
        if has_rope:
            x = x * c + pltpu.roll(x, LANES - ROPE_DIM // 2, 1) * sa + pltpu.roll(x, ROPE_DIM // 2, 1) * sb
        if scale != 1.0:
            x = x * scale
        for o in outs:
            o[:, sl] = x.astype(o.dtype)


def _head_post(p, col0, nh, t_len, out_dtypes, gain=None, rope=None, scale=1.0):
    n = p.shape[0]
    w = nh * HEAD_DIM
    assert col0 % w == 0
    tr = _pick(t_len, (256, 128, 64))
    tpb = t_len // tr
    in_specs = [pl.BlockSpec((tr, w), lambda i: (i, col0 // w))]
    args = [p]
    if gain is not None:
        in_specs.append(pl.BlockSpec((1, HEAD_DIM), lambda i: (0, 0)))
        args.append(gain.reshape(1, HEAD_DIM))
    if rope is not None:
        for tab in rope:
            in_specs.append(pl.BlockSpec((tr, LANES), lambda i: (i % tpb, 0)))
            args.append(tab)
    outs = pl.pallas_call(
        functools.partial(_head_body, nh=nh, has_norm=gain is not None, has_rope=rope is not None,
                          scale=scale, n_out=len(out_dtypes)),
        grid=(n // tr,),
        in_specs=in_specs,
        out_specs=[pl.BlockSpec((tr, w), lambda i: (i, 0)) for _ in out_dtypes],
        out_shape=[jax.ShapeDtypeStruct((n, w), dt) for dt in out_dtypes],
        compiler_params=_params("parallel"),
        name="head_post",
    )(*args)
    return outs


def _rope_tables(pos):
    half = ROPE_DIM // 2
    inv_freq = jnp.power(ROPE_THETA, -jnp.arange(half, dtype=F32) / half)
    ang = pos.astype(F32)[:, None] * inv_freq[None, :]
    cos, sin = jnp.cos(ang), jnp.sin(ang)
    t = pos.shape[0]
    zh = jnp.zeros((t, half), F32)
    zr = jnp.zeros((t, HEAD_DIM - ROPE_DIM), F32)
    c = jnp.concatenate([cos, cos, jnp.ones((t, HEAD_DIM - ROPE_DIM), F32)], axis=1)
    sa = jnp.concatenate([-sin, zh, zr], axis=1)
    sb = jnp.concatenate([zh, sin, zr], axis=1)
    return c, sa, sb


def _norm_heads(x, gain):
    ms = jnp.mean(x * x, axis=-1, keepdims=True)
    return x * lax.rsqrt(ms + EPS) * gain


def _rope_heads(x, c, sa, sb):
    return x * c + pltpu.roll(x, LANES - ROPE_DIM // 2, 1) * sa + pltpu.roll(x, ROPE_DIM // 2, 1) * sb


def _proj_heads_body(a_ref, b_ref, g_ref, c_ref, sa_ref, sb_ref, o_ref, *, has_norm, scale):
    p = lax.dot_general(a_ref[...], b_ref[...], _NN, preferred_element_type=F32)
    c, sa, sb = c_ref[...], sa_ref[...], sb_ref[...]
    for h in range(p.shape[1] // HEAD_DIM):
        sl = slice(h * HEAD_DIM, (h + 1) * HEAD_DIM)
        x = p[:, sl]
        if has_norm:
            x = _norm_heads(x, g_ref[...])
        x = _rope_heads(x, c, sa, sb)
        if scale != 1.0:
            x = x * scale
        o_ref[:, sl] = x.astype(o_ref.dtype)


def _proj_heads(a, b, col0, width, rope, tm, gain=None, scale=1.0):
    m, kdim = a.shape
    tn = _pick(math.gcd(width, col0) if col0 else width, (1024, 512, 256, 128))
    assert col0 % tn == 0 and width % tn == 0
    tpb = rope[0].shape[0] // tm
    g = jnp.ones((HEAD_DIM,), F32) if gain is None else gain
    tab_spec = pl.BlockSpec((tm, LANES), lambda i, j: (i % tpb, 0))
    return pl.pallas_call(
        functools.partial(_proj_heads_body, has_norm=gain is not None, scale=scale),
        grid=(m // tm, width // tn),
        in_specs=[pl.BlockSpec((tm, kdim), lambda i, j: (i, 0)),
                  pl.BlockSpec((kdim, tn), lambda i, j: (0, col0 // tn + j)),
                  pl.BlockSpec((1, HEAD_DIM), lambda i, j: (0, 0)),
                  tab_spec, tab_spec, tab_spec],
        out_specs=pl.BlockSpec((tm, tn), lambda i, j: (i, j)),
        out_shape=jax.ShapeDtypeStruct((m, width), BF16),
        compiler_params=_params("parallel", "parallel"),
        name="proj_heads",
    )(a, b, g.reshape(1, HEAD_DIM), *rope)


def _proj_glu_body(a_ref, ba_ref, bg_ref, o_ref):
    x = a_ref[...]
    pa = lax.dot_general(x, ba_ref[...], _NN, preferred_element_type=F32)
    pg = lax.dot_general(x, bg_ref[...], _NN, preferred_element_type=F32)
    o_ref[...] = pa * jax.nn.sigmoid(pg)


def _proj_glu(a, b, col_a, col_g, width, tm):
    m, kdim = a.shape
    tn = _pick(math.gcd(width, math.gcd(col_a, col_g)) if col_a or col_g else width, (512, 256, 128))
    assert col_a % tn == 0 and col_g % tn == 0 and width % tn == 0
    return pl.pallas_call(
        _proj_glu_body,
        grid=(m // tm, width // tn),
        in_specs=[pl.BlockSpec((tm, kdim), lambda i, j: (i, 0)),
                  pl.BlockSpec((kdim, tn), lambda i, j: (0, col_a // tn + j)),
                  pl.BlockSpec((kdim, tn), lambda i, j: (0, col_g // tn + j))],
        out_specs=pl.BlockSpec((tm, tn), lambda i, j: (i, j)),
        out_shape=jax.ShapeDtypeStruct((m, width), F32),
        compiler_params=_params("parallel", "parallel"),
        name="proj_glu",
    )(a, b, b)


KV_W = N_KV_HEADS * HEAD_DIM
TAIL_W = 2 * KV_W + IDX_DIM + LANES


def _proj_tail_body(a_ref, b_ref, g_ref, c_ref, sa_ref, sb_ref, k32_ref, k16_ref, v32_ref, v16_ref,
                    ki32_ref, ki16_ref, wi_ref):
    p = lax.dot_general(a_ref[...], b_ref[...], _NN, preferred_element_type=F32)
    c, sa, sb = c_ref[...], sa_ref[...], sb_ref[...]
    for h in range(N_KV_HEADS):
        sl = slice(h * HEAD_DIM, (h + 1) * HEAD_DIM)
        x = _rope_heads(_norm_heads(p[:, sl], g_ref[...]), c, sa, sb)
        k32_ref[:, h, :] = x
        k16_ref[:, sl] = x.astype(k16_ref.dtype)
    v = p[:, KV_W:2 * KV_W]
    for h in range(N_KV_HEADS):
        v32_ref[:, h, :] = v[:, h * HEAD_DIM:(h + 1) * HEAD_DIM]
    v16_ref[...] = v.astype(v16_ref.dtype)
    ki = _rope_heads(p[:, 2 * KV_W:2 * KV_W + IDX_DIM], c, sa, sb)
    ki32_ref[...] = ki
    ki16_ref[...] = ki.astype(ki16_ref.dtype)
    wi_ref[...] = p[:, 2 * KV_W + IDX_DIM:]


def _proj_tail(a, b, rope, tm, k_gain):
    m, kdim = a.shape
    assert b.shape[1] == TAIL_W
    tpb = rope[0].shape[0] // tm
    tab_spec = pl.BlockSpec((tm, LANES), lambda i: (i % tpb, 0))
    heads = (N_KV_HEADS, HEAD_DIM)
    shapes = (heads, (KV_W,), heads, (KV_W,), (IDX_DIM,), (IDX_DIM,), (LANES,))
    dtypes = (F32, BF16, F32, BF16, F32, BF16, F32)
    return pl.pallas_call(
        _proj_tail_body,
        grid=(m // tm,),
        in_specs=[pl.BlockSpec((tm, kdim), lambda i: (i, 0)),
                  pl.BlockSpec((kdim, TAIL_W), lambda i: (0, 0)),
                  pl.BlockSpec((1, HEAD_DIM), lambda i: (0, 0)),
                  tab_spec, tab_spec, tab_spec],
        out_specs=[pl.BlockSpec((tm,) + sh, lambda i, nd=len(sh): (i,) + (0,) * nd) for sh in shapes],
        out_shape=[jax.ShapeDtypeStruct((m,) + sh, dt) for sh, dt in zip(shapes, dtypes)],
        compiler_params=_params("parallel"),
        name="proj_tail",
    )(a, b, k_gain.reshape(1, HEAD_DIM), *rope)


CONV_HALO = 32
CONV_CW = 256
CONV_TB = 64
SUBLANES = 8


def _conv_body(u_ref, prev_ref, w_ref, b_ref, g_ref, beta_ref, o_ref, win_ref, rot_ref, c_ref, *, tt, ch):
    @pl.when(pl.program_id(1) == 0)
    def _():
        win_ref[0:CONV_HALO, :] = prev_ref[0]

    win_ref[CONV_HALO:CONV_HALO + tt, :] = u_ref[0]
    base = CONV_HALO - CONV_STATE
    for cc in range(ch // CONV_CW):
        cs = slice(cc * CONV_CW, (cc + 1) * CONV_CW)
        for r in range(SUBLANES):
            rows = SUBLANES * ((CONV_WIDTH - 1 - r) // SUBLANES) + tt
            rot_ref[r, 0:rows, :] = win_ref[base + r:base + r + rows, cs]
        for tb in range(tt // CONV_TB):
            acc = None
            for j in range(CONV_WIDTH):
                q, r = divmod(j, SUBLANES)
                lo = SUBLANES * q + tb * CONV_TB
                term = rot_ref[r, lo:lo + CONV_TB, :] * w_ref[j:j + 1, cs]
                acc = term if acc is None else acc + term
            c_ref[tb * CONV_TB:(tb + 1) * CONV_TB, cs] = acc + b_ref[:, cs]
    c = c_ref[...]
    mu = jnp.mean(c, axis=-1, keepdims=True)
    d = c - mu
    var = jnp.mean(d * d, axis=-1, keepdims=True)
    y = d * lax.rsqrt(var + EPS) * g_ref[...] + beta_ref[...]
    o_ref[0] = (y * jax.nn.sigmoid(y)).astype(o_ref.dtype)
    win_ref[0:CONV_HALO, :] = win_ref[tt:tt + CONV_HALO, :]


def _conv_module(u, prev, dw_w, dw_b, ln_g, ln_b):
    b, t, ch = u.shape
    tt = _pick(t, (256, 128, 64))
    prev_p = jnp.pad(prev, ((0, 0), (CONV_HALO - CONV_STATE, 0), (0, 0)))
    w_p = jnp.pad(dw_w, ((0, CONV_HALO - CONV_WIDTH), (0, 0)))
    row = lambda v: v.reshape(1, ch)
    rot_rows = SUBLANES * ((CONV_WIDTH - 1) // SUBLANES) + tt
    return pl.pallas_call(
        functools.partial(_conv_body, tt=tt, ch=ch),
        grid=(b, t // tt),
        in_specs=[pl.BlockSpec((1, tt, ch), lambda i, j: (i, j, 0)),
                  pl.BlockSpec((1, CONV_HALO, ch), lambda i, j: (i, 0, 0)),
                  pl.BlockSpec((CONV_HALO, ch), lambda i, j: (0, 0)),
                  pl.BlockSpec((1, ch), lambda i, j: (0, 0)),
                  pl.BlockSpec((1, ch), lambda i, j: (0, 0)),
                  pl.BlockSpec((1, ch), lambda i, j: (0, 0))],
        out_specs=pl.BlockSpec((1, tt, ch), lambda i, j: (i, j, 0)),
        out_shape=jax.ShapeDtypeStruct((b, t, ch), BF16),
        scratch_shapes=[pltpu.VMEM((tt + CONV_HALO, ch), F32),
                        pltpu.VMEM((SUBLANES, rot_rows, CONV_CW), F32),
                        pltpu.VMEM((tt, ch), F32)],
        compiler_params=_params("arbitrary", "arbitrary"),
        name="conv_module",
    )(u, prev_p, w_p, row(dw_b), row(ln_g), row(ln_b))


def _append_body(ck_ref, cv_ref, ci_ref, nk_ref, nv_ref, ni_ref, k_ref, v_ref, i_ref, *, n_past):
    c = pl.program_id(1)

    @pl.when(c < n_past)
    def _():
        for g in range(N_KV_HEADS):
            sl = slice(g * HEAD_DIM, (g + 1) * HEAD_DIM)
            k_ref[0, :, sl] = ck_ref[0, :, g, :].astype(k_ref.dtype)
            v_ref[0, :, sl] = cv_ref[0, :, g, :].astype(v_ref.dtype)
        i_ref[0] = ci_ref[0].astype(i_ref.dtype)

    @pl.when(c >= n_past)
    def _():
        k_ref[0] = nk_ref[0]
        v_ref[0] = nv_ref[0]
        i_ref[0] = ni_ref[0]


def _append_cache(k_past, v_past, ki_past, k_new, v_new, ki_new):
    b, past = k_past.shape[0], k_past.shape[1]
    sc = DSA_SC
    assert past % sc == 0 and k_new.shape[1] <= sc
    n_past = past // sc
    padw = ((0, 0), (0, sc - k_new.shape[1]), (0, 0))
    k_new, v_new, ki_new = jnp.pad(k_new, padw), jnp.pad(v_new, padw), jnp.pad(ki_new, padw)
    last = n_past - 1
    past4 = pl.BlockSpec((1, sc, N_KV_HEADS, HEAD_DIM), lambda i, c: (i, jnp.minimum(c, last), 0, 0))
    new3 = lambda w: pl.BlockSpec((1, sc, w), lambda i, c: (i, 0, 0))
    out3 = lambda w: pl.BlockSpec((1, sc, w), lambda i, c: (i, c, 0))
    s_pad = past + sc
    return pl.pallas_call(
        functools.partial(_append_body, n_past=n_past),
        grid=(b, n_past + 1),
        in_specs=[past4, past4, pl.BlockSpec((1, sc, IDX_DIM), lambda i, c: (i, jnp.minimum(c, last), 0)),
                  new3(KV_W), new3(KV_W), new3(IDX_DIM)],
        out_specs=[out3(KV_W), out3(KV_W), out3(IDX_DIM)],
        out_shape=[jax.ShapeDtypeStruct((b, s_pad, KV_W), BF16), jax.ShapeDtypeStruct((b, s_pad, KV_W), BF16),
                   jax.ShapeDtypeStruct((b, s_pad, IDX_DIM), BF16)],
        compiler_params=_params("parallel", "arbitrary"),
        name="append_cache",
    )(k_past, v_past, ki_past, k_new, v_new, ki_new)


DSA_SC = 512
DENOM_FLOOR = 1e-30


def _dsa_body(qs_ref, k_ref, v_ref, qi_ref, ki_ref, wi_ref, o_ref, key_ref, bias_ref, thr_ref, jcut_ref,
              qst_ref, mx_ref, acc_ref, kinf_ref, *, tq, nchunks, buckets, s_valid, topk, pos0, n_heads):
    sc = DSA_SC
    group = n_heads // N_KV_HEADS
    rows = group * tq
    q_lo = pos0 + pl.program_id(1) * tq
    qpos = q_lo + lax.broadcasted_iota(jnp.int32, (tq, 1), 0)
    chunk_shift = CHUNK.bit_length() - 1
    qchunk = qpos >> chunk_shift
    k_end = jnp.minimum((((q_lo + tq - 1) >> chunk_shift) + 1) << chunk_shift, s_valid)
    nact = jnp.minimum((k_end + sc - 1) >> (sc.bit_length() - 1), nchunks)
    nbucket = jnp.int32(buckets[-1])
    for bsz in reversed(buckets[:-1]):
        nbucket = jnp.where(nact <= bsz, bsz, nbucket)

    def score_chunk(c, carry):
        off = pl.multiple_of(c * sc, sc)
        ki_c = ki_ref[0, pl.ds(off, sc), :]
        acc = jnp.zeros((tq, sc), F32)
        for h in range(IDX_HEADS):
            s = lax.dot_general(qi_ref[0, :, h * IDX_DIM:(h + 1) * IDX_DIM], ki_c, _NT,
                                preferred_element_type=F32)
            acc = acc + wi_ref[0, :, h:h + 1] * jnp.maximum(s, 0.0)
        score = acc * IDX_SCALE
        spos = off + lax.broadcasted_iota(jnp.int32, (1, sc), 1)
        schunk = spos >> chunk_shift
        adm = jnp.logical_and(schunk <= qchunk, spos < s_valid)
        bits = pltpu.bitcast(score, jnp.int32)
        key = bits ^ ((bits >> 31) & 0x7FFFFFFF)
        key_ref[c] = jnp.where(adm, key, INT_MIN)
        return carry

    lax.fori_loop(0, nact, score_chunk, 0)

    def fill_chunk(c, carry):
        key_ref[c] = jnp.full((tq, sc), INT_MIN, jnp.int32)
        return carry

    lax.fori_loop(nact, nbucket, fill_chunk, 0)

    for bsz in buckets:
        @pl.when(nbucket == bsz)
        def _(bsz=bsz):
            def count(mask):
                return jnp.sum(jnp.sum(mask.astype(F32), axis=0), axis=-1, keepdims=True)

            def count_ge(cand):
                return count(key_ref[0:bsz] >= cand[None])

            zero = jnp.zeros((tq, 1), jnp.int32)
            ans = jnp.where(count_ge(zero) >= topk, zero, INT_MIN)

            def bit_step(i, ans):
                cand = ans | lax.shift_left(jnp.int32(1), 30 - i)
                return jnp.where(count_ge(cand) >= topk, cand, ans)

            ans = lax.fori_loop(0, 31, bit_step, ans)
            thr = jnp.maximum(ans, INT_MIN + 1)
            thr_ref[...] = thr
            need = topk - count(key_ref[0:bsz] > thr[None])
            ties = count(key_ref[0:bsz] == thr[None])
            nbits = (bsz * sc).bit_length()
            jcut_ref[...] = jnp.full((tq, 1), (1 << nbits) - 1, jnp.int32)

            @pl.when(jnp.max(ties - need) > 0.0)
            def _():
                pos = (lax.broadcasted_iota(jnp.int32, (bsz, tq, sc), 0) * sc
                       + lax.broadcasted_iota(jnp.int32, (bsz, tq, sc), 2))

                def ties_before(j):
                    keys = key_ref[0:bsz]
                    return count(jnp.where(keys == thr_ref[...][None], pos, (1 << nbits)) < j[None])

                def pos_step(i, j):
                    cand = j | lax.shift_left(jnp.int32(1), nbits - 1 - i)
                    return jnp.where(ties_before(cand) <= need, cand, j)

                jcut_ref[...] = lax.fori_loop(0, nbits, pos_step, zero)

    thr = thr_ref[...]
    jcut = jcut_ref[...]

    def bias_chunk(c, carry):
        spos = c * sc + lax.broadcasted_iota(jnp.int32, (1, sc), 1)
        bound = thr - jnp.where(spos < jcut, 1, 0)
        bias_ref[c] = jnp.where(key_ref[c] > bound, 0.0, NEG_BIG)
        return carry

    lax.fori_loop(0, nact, bias_chunk, 0)

    for hd in range(n_heads):
        qst_ref[hd * tq:(hd + 1) * tq, :] = qs_ref[0, :, hd * HEAD_DIM:(hd + 1) * HEAD_DIM]
    ntile = sc // LANES

    def masked_logits(c, g):
        off = pl.multiple_of(c * sc, sc)
        k_c = k_ref[0, pl.ds(off, sc), g * HEAD_DIM:(g + 1) * HEAD_DIM]
        lg = lax.dot_general(qst_ref[g * rows:(g + 1) * rows, :], k_c, _NT, preferred_element_type=F32)
        return (lg.reshape(group, tq, sc) + bias_ref[c][None]).reshape(rows, sc)

    def exact_row_max():
        mx_ref[...] = jnp.full(mx_ref.shape, NEG_BIG, F32)

        def max_chunk(c, carry):
            for g in range(N_KV_HEADS):
                rs = slice(g * rows, (g + 1) * rows)
                lg = masked_logits(c, g)
                mx = mx_ref[rs, :]
                for t in range(ntile):
                    mx = jnp.maximum(mx, lg[:, t * LANES:(t + 1) * LANES])
                mx_ref[rs, :] = mx
            return carry

        lax.fori_loop(0, nact, max_chunk, 0)
        mx_ref[...] = jnp.broadcast_to(jnp.max(mx_ref[...], axis=-1, keepdims=True), mx_ref.shape)

    ones = jnp.ones((sc, HEAD_DIM), BF16)

    def weighted_values():
        acc_ref[...] = jnp.zeros(acc_ref.shape, F32)

        def attn_chunk(c, carry):
            off = pl.multiple_of(c * sc, sc)
            for g in range(N_KV_HEADS):
                rs = slice(g * rows, (g + 1) * rows)
                lg = masked_logits(c, g)
                mb = mx_ref[rs, :]
                p = jnp.concatenate([jnp.exp(lg[:, t * LANES:(t + 1) * LANES] - mb) for t in range(ntile)], axis=1)
                v_ext = jnp.concatenate([v_ref[0, pl.ds(off, sc), g * HEAD_DIM:(g + 1) * HEAD_DIM], ones], axis=1)
                acc_ref[rs, :] += lax.dot_general(p.astype(BF16), v_ext, _NN, preferred_element_type=F32)
            return carry

        lax.fori_loop(0, nact, attn_chunk, 0)

    @pl.when(pl.program_id(1) == 0)
    def _():
        def kmax_chunk(c, km):
            off = pl.multiple_of(c * sc, sc)
            kc = jnp.abs(k_ref[0, pl.ds(off, sc), :].astype(F32))
            return jnp.maximum(km, jnp.max(kc, axis=0, keepdims=True))

        kinf_ref[...] = lax.fori_loop(0, nchunks, kmax_chunk, jnp.zeros((1, N_KV_HEADS * HEAD_DIM), F32))

    for g in range(N_KV_HEADS):
        rs = slice(g * rows, (g + 1) * rows)
        kinf = jnp.max(kinf_ref[:, g * HEAD_DIM:(g + 1) * HEAD_DIM], axis=-1, keepdims=True)
        q1 = jnp.sum(jnp.abs(qst_ref[rs, :].astype(F32)), axis=-1, keepdims=True)
        mx_ref[rs, :] = jnp.broadcast_to(q1 * kinf, (rows, LANES))
    weighted_values()

    @pl.when(jnp.logical_not(jnp.min(acc_ref[:, HEAD_DIM:2 * HEAD_DIM]) > DENOM_FLOOR))
    def _():
        exact_row_max()
        weighted_values()

    for hd in range(n_heads):
        hs = slice(hd * tq, (hd + 1) * tq)
        o = acc_ref[hs, 0:HEAD_DIM] / acc_ref[hs, HEAD_DIM:2 * HEAD_DIM]
        o_ref[0, :, hd * HEAD_DIM:(hd + 1) * HEAD_DIM] = o.astype(o_ref.dtype)


def _dsa_attention(qs, k_all, v_all, qi, ki_all, p3, wi_col, s_valid, topk, pos0):
    b, t, qw = qs.shape
    s_pad = k_all.shape[1]
    n_heads = qw // HEAD_DIM
    tq = _pick(t, (128, 64))
    nchunks = s_pad // DSA_SC
    if pos0 + tq >= s_valid:
        buckets = (nchunks,)
    else:
        sizes = [m << e for e in range(nchunks.bit_length()) for m in (1, 3)] + [nchunks]
        buckets = tuple(sorted({sz for sz in sizes if sz <= nchunks}))
    return pl.pallas_call(
        functools.partial(_dsa_body, tq=tq, nchunks=nchunks, buckets=buckets, s_valid=s_valid, topk=topk,
                          pos0=pos0, n_heads=n_heads),
        grid=(b, t // tq),
        in_specs=[pl.BlockSpec((1, tq, qw), lambda i, j: (i, j, 0)),
                  pl.BlockSpec((1, s_pad, N_KV_HEADS * HEAD_DIM), lambda i, j: (i, 0, 0)),
                  pl.BlockSpec((1, s_pad, N_KV_HEADS * HEAD_DIM), lambda i, j: (i, 0, 0)),
                  pl.BlockSpec((1, tq, IDX_HEADS * IDX_DIM), lambda i, j: (i, j, 0)),
                  pl.BlockSpec((1, s_pad, IDX_DIM), lambda i, j: (i, 0, 0)),
                  pl.BlockSpec((1, tq, LANES), lambda i, j: (i, j, wi_col // LANES))],
        out_specs=pl.BlockSpec((1, tq, qw), lambda i, j: (i, j, 0)),
        out_shape=jax.ShapeDtypeStruct((b, t, qw), BF16),
        scratch_shapes=[pltpu.VMEM((nchunks, tq, DSA_SC), jnp.int32),
                        pltpu.VMEM((nchunks, tq, DSA_SC), F32),
                        pltpu.VMEM((tq, 1), jnp.int32),
                        pltpu.VMEM((tq, 1), jnp.int32),
                        pltpu.VMEM((n_heads * tq, HEAD_DIM), BF16),
                        pltpu.VMEM((n_heads * tq, LANES), F32),
                        pltpu.VMEM((n_heads * tq, 2 * HEAD_DIM), F32),
                        pltpu.VMEM((1, N_KV_HEADS * HEAD_DIM), F32)],
        compiler_params=_params("parallel", "arbitrary"),
        name="dsa_attention",
    )(qs, k_all, v_all, qi, ki_all, p3)


def _memattn_body(q_ref, g_ref, mk_ref, mv_ref, o_ref):
    for h in range(MEM_HEADS):
        sl = slice(h * HEAD_DIM, (h + 1) * HEAD_DIM)
        x = q_ref[0, :, sl]
        ms = jnp.mean(x * x, axis=-1, keepdims=True)
        qn = (x * lax.rsqrt(ms + EPS) * g_ref[...] * ATTN_SCALE).astype(BF16)
        lg = lax.dot_general(qn, mk_ref[0, :, sl], _NT, preferred_element_type=F32)
        m = jnp.max(lg, axis=-1, keepdims=True)
        p = jnp.exp(lg - m)
        l = jnp.sum(p, axis=-1, keepdims=True)
        o = lax.dot_general(p.astype(BF16), mv_ref[0, :, sl], _NN, preferred_element_type=F32)
        o_ref[0, :, sl] = (o / l).astype(o_ref.dtype)


def _mem_attention(qm, gain, mk, mv):
    b, t, w = qm.shape
    m = mk.shape[1]
    tq = _pick(t, (512, 256, 128, 64))
    return pl.pallas_call(
        _memattn_body,
        grid=(b, t // tq),
        in_specs=[pl.BlockSpec((1, tq, w), lambda i, j: (i, j, 0)),
                  pl.BlockSpec((1, HEAD_DIM), lambda i, j: (0, 0)),
                  pl.BlockSpec((1, m, w), lambda i, j: (i, 0, 0)),
                  pl.BlockSpec((1, m, w), lambda i, j: (i, 0, 0))],
        out_specs=pl.BlockSpec((1, tq, w), lambda i, j: (i, j, 0)),
        out_shape=jax.ShapeDtypeStruct((b, t, w), BF16),
        compiler_params=_params("parallel", "parallel"),
        name="mem_attention",
    )(qm, gain.reshape(1, HEAD_DIM), mk, mv)


PEER_HALF = PEER_QDIM // 2
_PEER_PAIRS = [(a, b) for a in range(PEER_TOPK) for b in range(PEER_TOPK) if (a + 1) * (b + 1) <= PEER_TOPK]


def _route_body(pq_ref, k1_ref, k2_ref, r2_ref, e2_ref, na_ref, e1_ref, rk1_ref, v1_ref, v2_ref, c_ref, *, tt):
    row = lax.broadcasted_iota(jnp.int32, (PEER_KEYS, tt), 0)
    for h in range(PEER_HEADS):
        for table, (kref, vref, lo) in enumerate(((k1_ref, v1_ref, 0), (k2_ref, v2_ref, PEER_HALF))):
            q = pq_ref[:, h * PEER_QDIM + lo:h * PEER_QDIM + lo + PEER_HALF]
            s = lax.dot_general(kref[h], q, _NT, preferred_element_type=F32)
            cur = s
            rank = jnp.full((PEER_KEYS, tt), float(PEER_TOPK), F32)
            for r in range(PEER_TOPK):
                m = jnp.max(cur, axis=0, keepdims=True)
                vref[r, h:h + 1, :] = m
                first = jnp.min(jnp.where(cur == m, row, PEER_KEYS), axis=0, keepdims=True)
                hit = row == first
                cur = jnp.where(hit, -jnp.inf, cur)
                rank = jnp.where(hit, float(r), rank)
            ex = jnp.exp(s - vref[0, h:h + 1, :])
            if table == 0:
                rk1_ref[h] = rank
                e1_ref[h] = ex
            else:
                r2_ref[h] = rank.astype(r2_ref.dtype)
                e2_ref[h] = ex.astype(e2_ref.dtype)
    for p, (a, b) in enumerate(_PEER_PAIRS):
        c_ref[p] = v1_ref[a] + v2_ref[b]
    call = c_ref[...]

    def tau_step(p, tau):
        cp = c_ref[p]
        cnt = jnp.sum((call >= cp[None]).astype(F32), axis=0)
        return jnp.maximum(tau, jnp.where(cnt >= PEER_TOPK, cp, -jnp.inf))

    tau = lax.fori_loop(0, len(_PEER_PAIRS), tau_step, jnp.full((PEER_HEADS, tt), -jnp.inf, F32))
    zero = jnp.zeros((PEER_HEADS, tt), F32)
    cnt_gt = [zero] * PEER_TOPK
    cnt_eq = [zero] * PEER_TOPK
    for p, (a, b) in enumerate(_PEER_PAIRS):
        cnt_gt[a] = cnt_gt[a] + (call[p] > tau).astype(F32)
        cnt_eq[a] = cnt_eq[a] + (call[p] == tau).astype(F32)
    rem = float(PEER_TOPK) - sum(cnt_gt)
    n_of = []
    for a in range(PEER_TOPK):
        take = jnp.minimum(cnt_eq[a], rem)
        rem = rem - take
        n_of.append(cnt_gt[a] + take)
    z = zero
    for p, (a, b) in enumerate(_PEER_PAIRS):
        z = z + jnp.where(n_of[a] > float(b), jnp.exp(call[p] - call[0]), 0.0)
    inv_z = 1.0 / z
    for h in range(PEER_HEADS):
        rk = rk1_ref[h]
        na = jnp.zeros((PEER_KEYS, tt), F32)
        for a in range(PEER_TOPK):
            na = jnp.where(rk == float(a), n_of[a][h:h + 1, :], na)
        na_ref[h] = na
        e1_ref[h] = e1_ref[h] * inv_z[h:h + 1, :]


def _peer_route(pq, k1, k2):
    n = pq.shape[0]
    tt = _pick(n, (256, 128))
    shape = (PEER_HEADS, PEER_KEYS, n)
    big_spec = pl.BlockSpec((PEER_HEADS, PEER_KEYS, tt), lambda i: (0, 0, i))
    kspec = pl.BlockSpec((PEER_HEADS, PEER_KEYS, PEER_HALF), lambda i: (0, 0, 0))
    return pl.pallas_call(
        functools.partial(_route_body, tt=tt),
        grid=(n // tt,),
        in_specs=[pl.BlockSpec((tt, PEER_HEADS * PEER_QDIM), lambda i: (i, 0)), kspec, kspec],
        out_specs=[big_spec, big_spec, big_spec, big_spec],
        out_shape=[jax.ShapeDtypeStruct(shape, BF16), jax.ShapeDtypeStruct(shape, BF16),
                   jax.ShapeDtypeStruct(shape, F32), jax.ShapeDtypeStruct(shape, F32)],
        scratch_shapes=[pltpu.VMEM((PEER_HEADS, PEER_KEYS, tt), F32),
                        pltpu.VMEM((PEER_TOPK, PEER_HEADS, tt), F32),
                        pltpu.VMEM((PEER_TOPK, PEER_HEADS, tt), F32),
                        pltpu.VMEM((len(_PEER_PAIRS), PEER_HEADS, tt), F32)],
        compiler_params=_params("parallel"),
        name="peer_route",
    )(pq, k1, k2)


PEER_EB = 1024


def _coef_body(u_ref, x_ref, r2_ref, e2_ref, na_ref, e1_ref, o_ref, *, tt):
    nb = PEER_EB // PEER_KEYS
    e = pl.program_id(1)
    act = lax.dot_general(u_ref[...], x_ref[...], _NT, preferred_element_type=F32)
    gdt = r2_ref.dtype
    for j in range(nb):
        i1 = e * nb + j
        gate = jnp.zeros((PEER_KEYS, tt), gdt)
        for h in range(PEER_HEADS):
            na = jnp.broadcast_to(na_ref[h, pl.ds(i1, 1), :].astype(gdt), (PEER_KEYS, tt))
            e1 = jnp.broadcast_to(e1_ref[h, pl.ds(i1, 1), :].astype(gdt), (PEER_KEYS, tt))
            gate = gate + jnp.where(r2_ref[h] < na, e1 * e2_ref[h], jnp.zeros((), gdt))
        a = act[j * PEER_KEYS:(j + 1) * PEER_KEYS, :]
        gelu = 0.5 * a * (1.0 + lax.erf(a * math.sqrt(0.5)))
        o_ref[j * PEER_KEYS:(j + 1) * PEER_KEYS, :] = (gate * gelu.astype(gdt)).astype(o_ref.dtype)


def _peer_coef(u_tab, xn, r2, e2, na, e1):
    ne, d = u_tab.shape
    n = xn.shape[0]
    tt = _pick(n, (512, 256, 128))
    big_spec = pl.BlockSpec((PEER_HEADS, PEER_KEYS, tt), lambda i, e: (0, 0, i))
    return pl.pallas_call(
        functools.partial(_coef_body, tt=tt),
        grid=(n // tt, ne // PEER_EB),
        in_specs=[pl.BlockSpec((PEER_EB, d), lambda i, e: (e, 0)),
                  pl.BlockSpec((tt, d), lambda i, e: (i, 0)),
                  big_spec, big_spec, big_spec, big_spec],
        out_specs=pl.BlockSpec((PEER_EB, tt), lambda i, e: (e, i)),
        out_shape=jax.ShapeDtypeStruct((ne, n), BF16),
        compiler_params=_params("parallel", "arbitrary"),
        name="peer_coef",
    )(u_tab, xn, r2, e2, na, e1)


def _round_up(x, m):
    return -(-x // m) * m


def _layer(x, pos0, conv_prev, k_past, v_past, ki_past, mem_k, mem_v, w):
    b, t, d = x.shape
    n = b * t
    ch = d // 2
    n_heads = (d - ch) // HEAD_DIM
    off = w["offsets"]
    pos = pos0 + jnp.arange(t, dtype=jnp.int32)
    rope = _rope_tables(pos)

    x2 = x.reshape(n, d)
    hn = _rmsnorm(x2, w["norm_mix_g"])
    tm = _pick(n, (1024, 512, 256, 128, 64))
    if tm > t:
        rope = tuple(jnp.tile(tab, (tm // t, 1)) for tab in rope)
    u = _proj_glu(hn, w["w_in"], off["a"], off["g"], ch, tm)
    qs = _proj_heads(hn, w["w_in"], off["q"], n_heads * HEAD_DIM, rope, tm, gain=w["q_norm_g"], scale=ATTN_SCALE)
    qi = _proj_heads(hn, w["w_in"], off["qi"], IDX_HEADS * IDX_DIM, rope, tm)
    k32, k16, v32, v16, ki32, ki16, wi = _proj_tail(hn, w["w_tail"], rope, tm, w["k_norm_g"])
    kvw = KV_W

    u3 = u.reshape(b, t, ch)
    prev = jnp.zeros((b, CONV_STATE, ch), F32) if conv_prev is None else conv_prev
    conv_out = _conv_module(u3, prev, w["dw_w"], w["dw_b"], w["conv_ln_g"], w["conv_ln_b"])
    conv_new = u3[:, t - CONV_STATE:, :]

    k3, v3, ki3 = k16.reshape(b, t, kvw), v16.reshape(b, t, kvw), ki16.reshape(b, t, IDX_DIM)
    s_valid = t
    if k_past is not None:
        s_valid = k_past.shape[1] + t
        k3, v3, ki3 = _append_cache(k_past, v_past, ki_past, k3, v3, ki3)
    s_pad = _round_up(k3.shape[1], DSA_SC)
    if s_pad != k3.shape[1]:
        padw = ((0, 0), (0, s_pad - k3.shape[1]), (0, 0))
        k3, v3, ki3 = jnp.pad(k3, padw), jnp.pad(v3, padw), jnp.pad(ki3, padw)
    topk = min(TOPK_MAX, s_valid // 4)
    attn = _dsa_attention(qs.reshape(b, t, n_heads * HEAD_DIM), k3, v3, qi.reshape(b, t, IDX_HEADS * IDX_DIM),
                          ki3, wi.reshape(b, t, LANES), 0, s_valid, topk, pos0)

    h1 = _matmul_cat(conv_out.reshape(n, ch), attn.reshape(n, n_heads * HEAD_DIM), w["w_out"], x2)

    hn2 = _rmsnorm(h1, w["norm_mem_g"])
    qm = _matmul(hn2, w["w_q_mem"])
    om = _mem_attention(qm.reshape(b, t, MEM_DIM), w["mem_q_norm_g"], mem_k, mem_v)
    h2 = _matmul(om.reshape(n, MEM_DIM), w["w_o_mem"], res=h1)

    hn3 = _rmsnorm(h2, w["norm_ffn_g"])
    pq = _matmul(hn3, w["peer_wq"], out_dtype=BF16)
    r2, e2, na, e1 = _peer_route(pq, w["peer_sub_k1"], w["peer_sub_k2"])
    coef_t = _peer_coef(w["peer_u"], hn3, r2, e2, na, e1)
    ne = coef_t.shape[0]
    y = _matmul(coef_t, w["peer_v"], res=h2, trans_a=True,
                tm=_pick(n, (1024, 512)), tn=_pick(d, (1024, 512)), tk=_pick(ne, (2048, 1024)))

    return (y.reshape(b, t, d), k32.reshape(b, t, N_KV_HEADS, HEAD_DIM), v32.reshape(b, t, N_KV_HEADS, HEAD_DIM),
            ki32.reshape(b, t, IDX_DIM), conv_new)


def _mem_kv(mem, mem_norm_g, w_kv, mem_k_norm_g):
    b, m, d = mem.shape
    mn = _rmsnorm(mem.reshape(b * m, d), mem_norm_g)
    mkv = _matmul(mn, w_kv)
    mk32, mk16 = _head_post(mkv, 0, MEM_HEADS, m, (F32, BF16), gain=mem_k_norm_g)
    mv32 = mkv[:, MEM_DIM:]
    return mk32.reshape(b, m, MEM_DIM), mv32.reshape(b, m, MEM_DIM), mk16.reshape(b, m, MEM_DIM)


def _prep_layer_weights(l, d, norm_mix_g, w_in, dw_w, dw_b, conv_ln_g, conv_ln_b, q_norm_g, k_norm_g, w_out,
                        norm_mem_g, w_q_mem, mem_q_norm_g, w_o_mem, norm_ffn_g, peer_wq, peer_sub_k1,
                        peer_sub_k2, peer_u, peer_v):
    ch = d // 2
    nq = d - ch
    niq = IDX_HEADS * IDX_DIM
    offsets = {"a": 0, "g": ch, "q": 2 * ch}
    offsets["k"] = offsets["q"] + nq
    offsets["qi"] = offsets["k"] + 2 * KV_W
    offsets["ki"] = offsets["qi"] + niq
    wl = w_in[l].astype(BF16)
    end = offsets["ki"] + IDX_DIM + IDX_HEADS
    w_tail = jnp.concatenate([wl[:, offsets["k"]:offsets["qi"]], wl[:, offsets["ki"]:end],
                              jnp.zeros((d, LANES - IDX_HEADS), BF16)], axis=1)
    return {
        "offsets": offsets,
        "w_in": wl,
        "w_tail": w_tail,
        "norm_mix_g": norm_mix_g[l], "dw_w": dw_w[l], "dw_b": dw_b[l],
        "conv_ln_g": conv_ln_g[l], "conv_ln_b": conv_ln_b[l],
        "q_norm_g": q_norm_g[l], "k_norm_g": k_norm_g[l],
        "w_out": w_out[l].astype(BF16),
        "norm_mem_g": norm_mem_g[l], "w_q_mem": w_q_mem[l].astype(BF16), "mem_q_norm_g": mem_q_norm_g[l],
        "w_o_mem": w_o_mem[l].astype(BF16),
        "norm_ffn_g": norm_ffn_g[l], "peer_wq": peer_wq[l].astype(BF16),
        "peer_sub_k1": peer_sub_k1[l].astype(BF16), "peer_sub_k2": peer_sub_k2[l].astype(BF16),
        "peer_u": peer_u[l].astype(BF16), "peer_v": peer_v[l].astype(BF16),
    }


def kernel(x_prompt, x_sample, mem_prompt, cache_k, cache_v, cache_k_idx, state_conv, cache_mem_k, cache_mem_v, norm_mix_g, w_in, dw_w, dw_b, conv_ln_g, conv_ln_b, q_norm_g, k_norm_g, w_out, norm_mem_g, mem_norm_g, w_q_mem, w_k_mem, w_v_mem, mem_q_norm_g, mem_k_norm_g, w_o_mem, norm_ffn_g, peer_wq, peer_sub_k1, peer_sub_k2, peer_u, peer_v):
    depth = w_in.shape[0]
    d = x_prompt.shape[-1]
    past_len = cache_k.shape[2]
    h_p, h_s = x_prompt, x_sample
    outs = [[] for _ in range(10)]
    for l in range(depth):
        w = _prep_layer_weights(l, d, norm_mix_g, w_in, dw_w, dw_b, conv_ln_g, conv_ln_b, q_norm_g, k_norm_g,
                                w_out, norm_mem_g, w_q_mem, mem_q_norm_g, w_o_mem, norm_ffn_g, peer_wq,
                                peer_sub_k1, peer_sub_k2, peer_u, peer_v)
        w_kv = jnp.concatenate([w_k_mem[l], w_v_mem[l]], axis=1).astype(BF16)
        mk32, mv32, mk16 = _mem_kv(mem_prompt, mem_norm_g[l], w_kv, mem_k_norm_g[l])
        h_p, kp, vp, kip, cp = _layer(h_p, 0, None, None, None, None, mk16, mv32.astype(BF16), w)
        bs, ms = cache_mem_k.shape[1], cache_mem_k.shape[2]
        h_s, ks, vs, kis, cs = _layer(h_s, past_len, state_conv[l], cache_k[l], cache_v[l], cache_k_idx[l],
                                      cache_mem_k[l].reshape(bs, ms, MEM_DIM).astype(BF16),
                                      cache_mem_v[l].reshape(bs, ms, MEM_DIM).astype(BF16), w)
        bp, mp = mem_prompt.shape[0], mem_prompt.shape[1]
        for lst, val in zip(outs, (kp, vp, kip, cp, mk32.reshape(bp, mp, MEM_HEADS, HEAD_DIM),
                                   mv32.reshape(bp, mp, MEM_HEADS, HEAD_DIM), ks, vs, kis, cs)):
            lst.append(val)
    return (h_p, h_s) + tuple(jnp.stack(lst) for lst in outs)
```

```python
import functools
import math

import jax
import jax.numpy as jnp
from jax import lax
from jax.experimental import pallas as pl
from jax.experimental.pallas import tpu as pltpu

F32 = jnp.float32
BF16 = jnp.bfloat16

CHUNK = 64
CONV_WIDTH = 31
CONV_STATE = CONV_WIDTH - 1
HEAD_DIM = 128
N_KV_HEADS = 4
ROPE_DIM = HEAD_DIM // 4
ROPE_THETA = 500000.0
IDX_HEADS = 32
IDX_DIM = 128
IDX_SCALE = (IDX_HEADS ** -0.5) * (IDX_DIM ** -0.5)
TOPK_MAX = 256
ATTN_SCALE = HEAD_DIM ** -0.5
MEM_HEADS = 4
MEM_DIM = MEM_HEADS * HEAD_DIM
PEER_KEYS = 128
PEER_HEADS = 8
PEER_QDIM = 256
PEER_TOPK = 16
EPS = 1e-6

LANES = 128
VMEM_LIMIT_BYTES = 56 * 1024 * 1024

NEG_BIG = -1e30
INT_MIN = -(2 ** 31)

_NT = (((1,), (1,)), ((), ()))
_NN = (((1,), (0,)), ((), ()))
_TN = (((0,), (0,)), ((), ()))


def _params(*sem):
    return pltpu.CompilerParams(dimension_semantics=sem, vmem_limit_bytes=VMEM_LIMIT_BYTES)


def _pick(n, candidates):
    for c in candidates:
        if c <= n and n % c == 0:
            return c
    return n


def _rmsnorm_body(x_ref, g_ref, o_ref):
    x = x_ref[...]
    ms = jnp.mean(x * x, axis=-1, keepdims=True)
    o_ref[...] = (x * lax.rsqrt(ms + EPS) * g_ref[...]).astype(o_ref.dtype)


def _rmsnorm(x, g):
    n, d = x.shape
    tr = _pick(n, (256, 128, 64))
    return pl.pallas_call(
        _rmsnorm_body,
        grid=(n // tr,),
        in_specs=[pl.BlockSpec((tr, d), lambda i: (i, 0)),
                  pl.BlockSpec((1, d), lambda i: (0, 0))],
        out_specs=pl.BlockSpec((tr, d), lambda i: (i, 0)),
        out_shape=jax.ShapeDtypeStruct((n, d), BF16),
        compiler_params=_params("parallel"),
        name="rmsnorm",
    )(x, g.reshape(1, d))


def _mm_body(*refs, nk, dims, has_res):
    a_ref, b_ref = refs[0], refs[1]
    r_ref = refs[2] if has_res else None
    o_ref = refs[2 + has_res]
    p = lax.dot_general(a_ref[...], b_ref[...], dims, preferred_element_type=F32)
    if nk == 1:
        if has_res:
            p = r_ref[...] + p
        o_ref[...] = p.astype(o_ref.dtype)
        return
    k = pl.program_id(2)

    @pl.when(k == 0)
    def _():
        o_ref[...] = r_ref[...] + p if has_res else p

    @pl.when(k > 0)
    def _():
        o_ref[...] += p


def _matmul(a, b, res=None, out_dtype=F32, trans_a=False, tm=None, tn=None, tk=None):
    if trans_a:
        kdim, m = a.shape
    else:
        m, kdim = a.shape
    n = b.shape[1]
    tm = tm or _pick(m, (1024, 512, 256, 128, 64))
    tn = tn or _pick(n, (768, 512, 384, 256, 128))
    tk = tk or kdim
    nk = kdim // tk
    if trans_a:
        a_spec = pl.BlockSpec((tk, tm), lambda i, j, k: (k, i))
        dims = _TN
    else:
        a_spec = pl.BlockSpec((tm, tk), lambda i, j, k: (i, k))
        dims = _NN
    in_specs = [a_spec, pl.BlockSpec((tk, tn), lambda i, j, k: (k, j))]
    args = [a, b]
    if res is not None:
        in_specs.append(pl.BlockSpec((tm, tn), lambda i, j, k: (i, j)))
        args.append(res)
    assert nk == 1 or out_dtype == F32
    return pl.pallas_call(
        functools.partial(_mm_body, nk=nk, dims=dims, has_res=res is not None),
        grid=(m // tm, n // tn, nk),
        in_specs=in_specs,
        out_specs=pl.BlockSpec((tm, tn), lambda i, j, k: (i, j)),
        out_shape=jax.ShapeDtypeStruct((m, n), out_dtype),
        compiler_params=_params("parallel", "parallel", "arbitrary"),
        name="matmul",
    )(*args)


def _mm2_body(a1_ref, a2_ref, b_ref, r_ref, o_ref):
    k1 = a1_ref.shape[1]
    p = lax.dot_general(a1_ref[...], b_ref[0:k1, :], _NN, preferred_element_type=F32)
    p = p + lax.dot_general(a2_ref[...], b_ref[k1:, :], _NN, preferred_element_type=F32)
    o_ref[...] = (r_ref[...] + p).astype(o_ref.dtype)


def _matmul_cat(a1, a2, b, res):
    m, k1 = a1.shape
    k2 = a2.shape[1]
    n = b.shape[1]
    tm = _pick(m, (1024, 512, 256, 128, 64))
    tn = _pick(n, (768, 512, 384, 256, 128))
    return pl.pallas_call(
        _mm2_body,
        grid=(m // tm, n // tn),
        in_specs=[pl.BlockSpec((tm, k1), lambda i, j: (i, 0)),
                  pl.BlockSpec((tm, k2), lambda i, j: (i, 0)),
                  pl.BlockSpec((k1 + k2, tn), lambda i, j: (0, j)),
                  pl.BlockSpec((tm, tn), lambda i, j: (i, j))],
        out_specs=pl.BlockSpec((tm, tn), lambda i, j: (i, j)),
        out_shape=jax.ShapeDtypeStruct((m, n), F32),
        compiler_params=_params("parallel", "parallel"),
        name="matmul_cat",
    )(a1, a2, b, res)


def _head_body(*refs, nh, has_norm, has_rope, scale, n_out):
    it = iter(refs)
    p_ref = next(it)
    g_ref = next(it) if has_norm else None
    if has_rope:
        c = next(it)[...]
        sa = next(it)[...]
        sb = next(it)[...]
    outs = [next(it) for _ in range(n_out)]
    for h in range(nh):
        sl = slice(h * HEAD_DIM, (h + 1) * HEAD_DIM)
        x = p_ref[:, sl]
        if has_norm:
            ms = jnp.mean(x * x, axis=-1, keepdims=True)
            x = x * lax.rsqrt(ms + EPS) * g_ref[...]
        if has_rope:
            x = x * c + pltpu.roll(x, LANES - ROPE_DIM // 2, 1) * sa + pltpu.roll(x, ROPE_DIM // 2, 1) * sb
        if scale != 1.0:
            x = x * scale
        for o in outs:
            o[:, sl] = x.astype(o.dtype)


def _head_post(p, col0, nh, t_len, out_dtypes, gain=None, rope=None, scale=1.0):
    n = p.shape[0]
    w = nh * HEAD_DIM
    assert col0 % w == 0
    tr = _pick(t_len, (256, 128, 64))
    tpb = t_len // tr
    in_specs = [pl.BlockSpec((tr, w), lambda i: (i, col0 // w))]
    args = [p]
    if gain is not None:
        in_specs.append(pl.BlockSpec((1, HEAD_DIM), lambda i: (0, 0)))
        args.append(gain.reshape(1, HEAD_DIM))
    if rope is not None:
        for tab in rope:
            in_specs.append(pl.BlockSpec((tr, LANES), lambda i: (i % tpb, 0)))
            args.append(tab)
    outs = pl.pallas_call(
        functools.partial(_head_body, nh=nh, has_norm=gain is not None, has_rope=rope is not None,
                          scale=scale, n_out=len(out_dtypes)),
        grid=(n // tr,),
        in_specs=in_specs,
        out_specs=[pl.BlockSpec((tr, w), lambda i: (i, 0)) for _ in out_dtypes],
        out_shape=[jax.ShapeDtypeStruct((n, w), dt) for dt in out_dtypes],
        compiler_params=_params("parallel"),
        name="head_post",
    )(*args)
    return outs


def _rope_tables(pos):
    half = ROPE_DIM // 2
    inv_freq = jnp.power(ROPE_THETA, -jnp.arange(half, dtype=F32) / half)
    ang = pos.astype(F32)[:, None] * inv_freq[None, :]
    cos, sin = jnp.cos(ang), jnp.sin(ang)
    t = pos.shape[0]
    zh = jnp.zeros((t, half), F32)
    zr = jnp.zeros((t, HEAD_DIM - ROPE_DIM), F32)
    c = jnp.concatenate([cos, cos, jnp.ones((t, HEAD_DIM - ROPE_DIM), F32)], axis=1)
    sa = jnp.concatenate([-sin, zh, zr], axis=1)
    sb = jnp.concatenate([zh, sin, zr], axis=1)
    return c, sa, sb


def _norm_heads(x, gain):
    ms = jnp.mean(x * x, axis=-1, keepdims=True)
    return x * lax.rsqrt(ms + EPS) * gain


def _rope_heads(x, c, sa, sb):
    return x * c + pltpu.roll(x, LANES - ROPE_DIM // 2, 1) * sa + pltpu.roll(x, ROPE_DIM // 2, 1) * sb


def _proj_heads_body(a_ref, b_ref, g_ref, c_ref, sa_ref, sb_ref, o_ref, *, has_norm, scale):
    p = lax.dot_general(a_ref[...], b_ref[...], _NN, preferred_element_type=F32)
    c, sa, sb = c_ref[...], sa_ref[...], sb_ref[...]
    for h in range(p.shape[1] // HEAD_DIM):
        sl = slice(h * HEAD_DIM, (h + 1) * HEAD_DIM)
        x = p[:, sl]
        if has_norm:
            x = _norm_heads(x, g_ref[...])
        x = _rope_heads(x, c, sa, sb)
        if scale != 1.0:
            x = x * scale
        o_ref[:, sl] = x.astype(o_ref.dtype)


def _proj_heads(a, b, col0, width, rope, tm, gain=None, scale=1.0):
    m, kdim = a.shape
    tn = _pick(math.gcd(width, col0) if col0 else width, (1024, 512, 256, 128))
    assert col0 % tn == 0 and width % tn == 0
    tpb = rope[0].shape[0] // tm
    g = jnp.ones((HEAD_DIM,), F32) if gain is None else gain
    tab_spec = pl.BlockSpec((tm, LANES), lambda i, j: (i % tpb, 0))
    return pl.pallas_call(
        functools.partial(_proj_heads_body, has_norm=gain is not None, scale=scale),
        grid=(m // tm, width // tn),
        in_specs=[pl.BlockSpec((tm, kdim), lambda i, j: (i, 0)),
                  pl.BlockSpec((kdim, tn), lambda i, j: (0, col0 // tn + j)),
                  pl.BlockSpec((1, HEAD_DIM), lambda i, j: (0, 0)),
                  tab_spec, tab_spec, tab_spec],
        out_specs=pl.BlockSpec((tm, tn), lambda i, j: (i, j)),
        out_shape=jax.ShapeDtypeStruct((m, width), BF16),
        compiler_params=_params("parallel", "parallel"),
        name="proj_heads",
    )(a, b, g.reshape(1, HEAD_DIM), *rope)


def _proj_glu_body(a_ref, ba_ref, bg_ref, o_ref):
    x = a_ref[...]
    pa = lax.dot_general(x, ba_ref[...], _NN, preferred_element_type=F32)
    pg = lax.dot_general(x, bg_ref[...], _NN, preferred_element_type=F32)
    o_ref[...] = pa * jax.nn.sigmoid(pg)


def _proj_glu(a, b, col_a, col_g, width, tm):
    m, kdim = a.shape
    tn = _pick(math.gcd(width, math.gcd(col_a, col_g)) if col_a or col_g else width, (512, 256, 128))
    assert col_a % tn == 0 and col_g % tn == 0 and width % tn == 0
    return pl.pallas_call(
        _proj_glu_body,
        grid=(m // tm, width // tn),
        in_specs=[pl.BlockSpec((tm, kdim), lambda i, j: (i, 0)),
                  pl.BlockSpec((kdim, tn), lambda i, j: (0, col_a // tn + j)),
                  pl.BlockSpec((kdim, tn), lambda i, j: (0, col_g // tn + j))],
        out_specs=pl.BlockSpec((tm, tn), lambda i, j: (i, j)),
        out_shape=jax.ShapeDtypeStruct((m, width), F32),
        compiler_params=_params("parallel", "parallel"),
        name="proj_glu",
    )(a, b, b)


KV_W = N_KV_HEADS * HEAD_DIM
TAIL_W = 2 * KV_W + IDX_DIM + LANES


def _proj_tail_body(a_ref, b_ref, g_ref, c_ref, sa_ref, sb_ref, k32_ref, k16_ref, v32_ref, v16_ref,
                    ki32_ref, ki16_ref, wi_ref):
    p = lax.dot_general(a_ref[...], b_ref[...], _NN, preferred_element_type=F32)
    c, sa, sb = c_ref[...], sa_ref[...], sb_ref[...]
    for h in range(N_KV_HEADS):
        sl = slice(h * HEAD_DIM, (h + 1) * HEAD_DIM)
        x = _rope_heads(_norm_heads(p[:, sl], g_ref[...]), c, sa, sb)
        k32_ref[:, h, :] = x
        k16_ref[:, sl] = x.astype(k16_ref.dtype)
    v = p[:, KV_W:2 * KV_W]
    for h in range(N_KV_HEADS):
        v32_ref[:, h, :] = v[:, h * HEAD_DIM:(h + 1) * HEAD_DIM]
    v16_ref[...] = v.astype(v16_ref.dtype)
    ki = _rope_heads(p[:, 2 * KV_W:2 * KV_W + IDX_DIM], c, sa, sb)
    ki32_ref[...] = ki
    ki16_ref[...] = ki.astype(ki16_ref.dtype)
    wi_ref[...] = p[:, 2 * KV_W + IDX_DIM:]


def _proj_tail(a, b, rope, tm, k_gain):
    m, kdim = a.shape
    assert b.shape[1] == TAIL_W
    tpb = rope[0].shape[0] // tm
    tab_spec = pl.BlockSpec((tm, LANES), lambda i: (i % tpb, 0))
    heads = (N_KV_HEADS, HEAD_DIM)
    shapes = (heads, (KV_W,), heads, (KV_W,), (IDX_DIM,), (IDX_DIM,), (LANES,))
    dtypes = (F32, BF16, F32, BF16, F32, BF16, F32)
    return pl.pallas_call(
        _proj_tail_body,
        grid=(m // tm,),
        in_specs=[pl.BlockSpec((tm, kdim), lambda i: (i, 0)),
                  pl.BlockSpec((kdim, TAIL_W), lambda i: (0, 0)),
                  pl.BlockSpec((1, HEAD_DIM), lambda i: (0, 0)),
                  tab_spec, tab_spec, tab_spec],
        out_specs=[pl.BlockSpec((tm,) + sh, lambda i, nd=len(sh): (i,) + (0,) * nd) for sh in shapes],
        out_shape=[jax.ShapeDtypeStruct((m,) + sh, dt) for sh, dt in zip(shapes, dtypes)],
        compiler_params=_params("parallel"),
        name="proj_tail",
    )(a, b, k_gain.reshape(1, HEAD_DIM), *rope)


CONV_HALO = 32
CONV_CW = 256
CONV_TB = 64
SUBLANES = 8


def _conv_body(u_ref, prev_ref, w_ref, b_ref, g_ref, beta_ref, o_ref, win_ref, rot_ref, c_ref, *, tt, ch):
    @pl.when(pl.program_id(1) == 0)
    def _():
        win_ref[0:CONV_HALO, :] = prev_ref[0]

    win_ref[CONV_HALO:CONV_HALO + tt, :] = u_ref[0]
    base = CONV_HALO - CONV_STATE
    for cc in range(ch // CONV_CW):
        cs = slice(cc * CONV_CW, (cc + 1) * CONV_CW)
        for r in range(SUBLANES):
            rows = SUBLANES * ((CONV_WIDTH - 1 - r) // SUBLANES) + tt
            rot_ref[r, 0:rows, :] = win_ref[base + r:base + r + rows, cs]
        for tb in range(tt // CONV_TB):
            acc = None
            for j in range(CONV_WIDTH):
                q, r = divmod(j, SUBLANES)
                lo = SUBLANES * q + tb * CONV_TB
                term = rot_ref[r, lo:lo + CONV_TB, :] * w_ref[j:j + 1, cs]
                acc = term if acc is None else acc + term
            c_ref[tb * CONV_TB:(tb + 1) * CONV_TB, cs] = acc + b_ref[:, cs]
    c = c_ref[...]
    mu = jnp.mean(c, axis=-1, keepdims=True)
    d = c - mu
    var = jnp.mean(d * d, axis=-1, keepdims=True)
    y = d * lax.rsqrt(var + EPS) * g_ref[...] + beta_ref[...]
    o_ref[0] = (y * jax.nn.sigmoid(y)).astype(o_ref.dtype)
    win_ref[0:CONV_HALO, :] = win_ref[tt:tt + CONV_HALO, :]


def _conv_module(u, prev, dw_w, dw_b, ln_g, ln_b):
    b, t, ch = u.shape
    tt = _pick(t, (256, 128, 64))
    prev_p = jnp.pad(prev, ((0, 0), (CONV_HALO - CONV_STATE, 0), (0, 0)))
    w_p = jnp.pad(dw_w, ((0, CONV_HALO - CONV_WIDTH), (0, 0)))
    row = lambda v: v.reshape(1, ch)
    rot_rows = SUBLANES * ((CONV_WIDTH - 1) // SUBLANES) + tt
    return pl.pallas_call(
        functools.partial(_conv_body, tt=tt, ch=ch),
        grid=(b, t // tt),
        in_specs=[pl.BlockSpec((1, tt, ch), lambda i, j: (i, j, 0)),
                  pl.BlockSpec((1, CONV_HALO, ch), lambda i, j: (i, 0, 0)),
                  pl.BlockSpec((CONV_HALO, ch), lambda i, j: (0, 0)),
                  pl.BlockSpec((1, ch), lambda i, j: (0, 0)),
                  pl.BlockSpec((1, ch), lambda i, j: (0, 0)),
                  pl.BlockSpec((1, ch), lambda i, j: (0, 0))],
        out_specs=pl.BlockSpec((1, tt, ch), lambda i, j: (i, j, 0)),
        out_shape=jax.ShapeDtypeStruct((b, t, ch), BF16),
        scratch_shapes=[pltpu.VMEM((tt + CONV_HALO, ch), F32),
                        pltpu.VMEM((SUBLANES, rot_rows, CONV_CW), F32),
                        pltpu.VMEM((tt, ch), F32)],
        compiler_params=_params("arbitrary", "arbitrary"),
        name="conv_module",
    )(u, prev_p, w_p, row(dw_b), row(ln_g), row(ln_b))


def _append_body(ck_ref, cv_ref, ci_ref, nk_ref, nv_ref, ni_ref, k_ref, v_ref, i_ref, *, n_past):
    c = pl.program_id(1)

    @pl.when(c < n_past)
    def _():
        for g in range(N_KV_HEADS):
            sl = slice(g * HEAD_DIM, (g + 1) * HEAD_DIM)
            k_ref[0, :, sl] = ck_ref[0, :, g, :].astype(k_ref.dtype)
            v_ref[0, :, sl] = cv_ref[0, :, g, :].astype(v_ref.dtype)
        i_ref[0] = ci_ref[0].astype(i_ref.dtype)

    @pl.when(c >= n_past)
    def _():
        k_ref[0] = nk_ref[0]
        v_ref[0] = nv_ref[0]
        i_ref[0] = ni_ref[0]


def _append_cache(k_past, v_past, ki_past, k_new, v_new, ki_new):
    b, past = k_past.shape[0], k_past.shape[1]
    sc = DSA_SC
    assert past % sc == 0 and k_new.shape[1] <= sc
    n_past = past // sc
    padw = ((0, 0), (0, sc - k_new.shape[1]), (0, 0))
    k_new, v_new, ki_new = jnp.pad(k_new, padw), jnp.pad(v_new, padw), jnp.pad(ki_new, padw)
    last = n_past - 1
    past4 = pl.BlockSpec((1, sc, N_KV_HEADS, HEAD_DIM), lambda i, c: (i, jnp.minimum(c, last), 0, 0))
    new3 = lambda w: pl.BlockSpec((1, sc, w), lambda i, c: (i, 0, 0))
    out3 = lambda w: pl.BlockSpec((1, sc, w), lambda i, c: (i, c, 0))
    s_pad = past + sc
    return pl.pallas_call(
        functools.partial(_append_body, n_past=n_past),
        grid=(b, n_past + 1),
        in_specs=[past4, past4, pl.BlockSpec((1, sc, IDX_DIM), lambda i, c: (i, jnp.minimum(c, last), 0)),
                  new3(KV_W), new3(KV_W), new3(IDX_DIM)],
        out_specs=[out3(KV_W), out3(KV_W), out3(IDX_DIM)],
        out_shape=[jax.ShapeDtypeStruct((b, s_pad, KV_W), BF16), jax.ShapeDtypeStruct((b, s_pad, KV_W), BF16),
                   jax.ShapeDtypeStruct((b, s_pad, IDX_DIM), BF16)],
        compiler_params=_params("parallel", "arbitrary"),
        name="append_cache",
    )(k_past, v_past, ki_past, k_new, v_new, ki_new)


DSA_SC = 512
DENOM_FLOOR = 1e-30


def _dsa_body(qs_ref, k_ref, v_ref, qi_ref, ki_ref, wi_ref, o_ref, key_ref, bias_ref, thr_ref, jcut_ref,
              qst_ref, mx_ref, acc_ref, kinf_ref, *, tq, nchunks, buckets, s_valid, topk, pos0, n_heads):
    sc = DSA_SC
    group = n_heads // N_KV_HEADS
    rows = group * tq
    q_lo = pos0 + pl.program_id(1) * tq
    qpos = q_lo + lax.broadcasted_iota(jnp.int32, (tq, 1), 0)
    chunk_shift = CHUNK.bit_length() - 1
    qchunk = qpos >> chunk_shift
    k_end = jnp.minimum((((q_lo + tq - 1) >> chunk_shift) + 1) << chunk_shift, s_valid)
    nact = jnp.minimum((k_end + sc - 1) >> (sc.bit_length() - 1), nchunks)
    nbucket = jnp.int32(buckets[-1])
    for bsz in reversed(buckets[:-1]):
        nbucket = jnp.where(nact <= bsz, bsz, nbucket)

    def score_chunk(c, carry):
        off = pl.multiple_of(c * sc, sc)
        ki_c = ki_ref[0, pl.ds(off, sc), :]
        acc = jnp.zeros((tq, sc), F32)
        for h in range(IDX_HEADS):
            s = lax.dot_general(qi_ref[0, :, h * IDX_DIM:(h + 1) * IDX_DIM], ki_c, _NT,
                                preferred_element_type=F32)
            acc = acc + wi_ref[0, :, h:h + 1] * jnp.maximum(s, 0.0)
        score = acc * IDX_SCALE
        spos = off + lax.broadcasted_iota(jnp.int32, (1, sc), 1)
        schunk = spos >> chunk_shift
        adm = jnp.logical_and(schunk <= qchunk, spos < s_valid)
        bits = pltpu.bitcast(score, jnp.int32)
        key = bits ^ ((bits >> 31) & 0x7FFFFFFF)
        key_ref[c] = jnp.where(adm, key, INT_MIN)
        return carry

    lax.fori_loop(0, nact, score_chunk, 0)

    def fill_chunk(c, carry):
        key_ref[c] = jnp.full((tq, sc), INT_MIN, jnp.int32)
        return carry

    lax.fori_loop(nact, nbucket, fill_chunk, 0)

    for bsz in buckets:
        @pl.when(nbucket == bsz)
        def _(bsz=bsz):
            def count(mask):
                return jnp.sum(jnp.sum(mask.astype(F32), axis=0), axis=-1, keepdims=True)

            def count_ge(cand):
                return count(key_ref[0:bsz] >= cand[None])

            zero = jnp.zeros((tq, 1), jnp.int32)
            ans = jnp.where(count_ge(zero) >= topk, zero, INT_MIN)

            def bit_step(i, ans):
                cand = ans | lax.shift_left(jnp.int32(1), 30 - i)
                return jnp.where(count_ge(cand) >= topk, cand, ans)

            ans = lax.fori_loop(0, 31, bit_step, ans)
            thr = jnp.maximum(ans, INT_MIN + 1)
            thr_ref[...] = thr
            need = topk - count(key_ref[0:bsz] > thr[None])
            ties = count(key_ref[0:bsz] == thr[None])
            nbits = (bsz * sc).bit_length()
            jcut_ref[...] = jnp.full((tq, 1), (1 << nbits) - 1, jnp.int32)

            @pl.when(jnp.max(ties - need) > 0.0)
            def _():
                pos = (lax.broadcasted_iota(jnp.int32, (bsz, tq, sc), 0) * sc
                       + lax.broadcasted_iota(jnp.int32, (bsz, tq, sc), 2))

                def ties_before(j):
                    keys = key_ref[0:bsz]
                    return count(jnp.where(keys == thr_ref[...][None], pos, (1 << nbits)) < j[None])

                def pos_step(i, j):
                    cand = j | lax.shift_left(jnp.int32(1), nbits - 1 - i)
                    return jnp.where(ties_before(cand) <= need, cand, j)

                jcut_ref[...] = lax.fori_loop(0, nbits, pos_step, zero)

    thr = thr_ref[...]
    jcut = jcut_ref[...]

    def bias_chunk(c, carry):
        spos = c * sc + lax.broadcasted_iota(jnp.int32, (1, sc), 1)
        bound = thr - jnp.where(spos < jcut, 1, 0)
        bias_ref[c] = jnp.where(key_ref[c] > bound, 0.0, NEG_BIG)
        return carry

    lax.fori_loop(0, nact, bias_chunk, 0)

    for hd in range(n_heads):
        qst_ref[hd * tq:(hd + 1) * tq, :] = qs_ref[0, :, hd * HEAD_DIM:(hd + 1) * HEAD_DIM]
    ntile = sc // LANES

    def masked_logits(c, g):
        off = pl.multiple_of(c * sc, sc)
        k_c = k_ref[0, pl.ds(off, sc), g * HEAD_DIM:(g + 1) * HEAD_DIM]
        lg = lax.dot_general(qst_ref[g * rows:(g + 1) * rows, :], k_c, _NT, preferred_element_type=F32)
        return (lg.reshape(group, tq, sc) + bias_ref[c][None]).reshape(rows, sc)

    def exact_row_max():
        mx_ref[...] = jnp.full(mx_ref.shape, NEG_BIG, F32)

        def max_chunk(c, carry):
            for g in range(N_KV_HEADS):
                rs = slice(g * rows, (g + 1) * rows)
                lg = masked_logits(c, g)
                mx = mx_ref[rs, :]
                for t in range(ntile):
                    mx = jnp.maximum(mx, lg[:, t * LANES:(t + 1) * LANES])
                mx_ref[rs, :] = mx
            return carry

        lax.fori_loop(0, nact, max_chunk, 0)
        mx_ref[...] = jnp.broadcast_to(jnp.max(mx_ref[...], axis=-1, keepdims=True), mx_ref.shape)

    ones = jnp.ones((sc, HEAD_DIM), BF16)

    def weighted_values():
        acc_ref[...] = jnp.zeros(acc_ref.shape, F32)

        def attn_chunk(c, carry):
            off = pl.multiple_of(c * sc, sc)
            for g in range(N_KV_HEADS):
                rs = slice(g * rows, (g + 1) * rows)
                lg = masked_logits(c, g)
                mb = mx_ref[rs, :]
                p = jnp.concatenate([jnp.exp(lg[:, t * LANES:(t + 1) * LANES] - mb) for t in range(ntile)], axis=1)
                v_ext = jnp.concatenate([v_ref[0, pl.ds(off, sc), g * HEAD_DIM:(g + 1) * HEAD_DIM], ones], axis=1)
                acc_ref[rs, :] += lax.dot_general(p.astype(BF16), v_ext, _NN, preferred_element_type=F32)
            return carry

        lax.fori_loop(0, nact, attn_chunk, 0)

    @pl.when(pl.program_id(1) == 0)
    def _():
        def kmax_chunk(c, km):
            off = pl.multiple_of(c * sc, sc)
            kc = jnp.abs(k_ref[0, pl.ds(off, sc), :].astype(F32))
            return jnp.maximum(km, jnp.max(kc, axis=0, keepdims=True))

        kinf_ref[...] = lax.fori_loop(0, nchunks, kmax_chunk, jnp.zeros((1, N_KV_HEADS * HEAD_DIM), F32))

    for g in range(N_KV_HEADS):
        rs = slice(g * rows, (g + 1) * rows)
        kinf = jnp.max(kinf_ref[:, g * HEAD_DIM:(g + 1) * HEAD_DIM], axis=-1, keepdims=True)
        q1 = jnp.sum(jnp.abs(qst_ref[rs, :].astype(F32)), axis=-1, keepdims=True)
        mx_ref[rs, :] = jnp.broadcast_to(q1 * kinf, (rows, LANES))
    weighted_values()

    @pl.when(jnp.logical_not(jnp.min(acc_ref[:, HEAD_DIM:2 * HEAD_DIM]) > DENOM_FLOOR))
    def _():
        exact_row_max()
        weighted_values()

    for hd in range(n_heads):
        hs = slice(hd * tq, (hd + 1) * tq)
        o = acc_ref[hs, 0:HEAD_DIM] / acc_ref[hs, HEAD_DIM:2 * HEAD_DIM]
        o_ref[0, :, hd * HEAD_DIM:(hd + 1) * HEAD_DIM] = o.astype(o_ref.dtype)


def _dsa_attention(qs, k_all, v_all, qi, ki_all, p3, wi_col, s_valid, topk, pos0):
    b, t, qw = qs.shape
    s_pad = k_all.shape[1]
    n_heads = qw // HEAD_DIM
    tq = _pick(t, (256, 128, 64))
    nchunks = s_pad // DSA_SC
    if pos0 + tq >= s_valid:
        buckets = (nchunks,)
    else:
        sizes = [m << e for e in range(nchunks.bit_length()) for m in (1, 3)] + [nchunks]
        buckets = tuple(sorted({sz for sz in sizes if sz <= nchunks}))
    return pl.pallas_call(
        functools.partial(_dsa_body, tq=tq, nchunks=nchunks, buckets=buckets, s_valid=s_valid, topk=topk,
                          pos0=pos0, n_heads=n_heads),
        grid=(b, t // tq),
        in_specs=[pl.BlockSpec((1, tq, qw), lambda i, j: (i, j, 0)),
                  pl.BlockSpec((1, s_pad, N_KV_HEADS * HEAD_DIM), lambda i, j: (i, 0, 0)),
                  pl.BlockSpec((1, s_pad, N_KV_HEADS * HEAD_DIM), lambda i, j: (i, 0, 0)),
                  pl.BlockSpec((1, tq, IDX_HEADS * IDX_DIM), lambda i, j: (i, j, 0)),
                  pl.BlockSpec((1, s_pad, IDX_DIM), lambda i, j: (i, 0, 0)),
                  pl.BlockSpec((1, tq, LANES), lambda i, j: (i, j, wi_col // LANES))],
        out_specs=pl.BlockSpec((1, tq, qw), lambda i, j: (i, j, 0)),
        out_shape=jax.ShapeDtypeStruct((b, t, qw), BF16),
        scratch_shapes=[pltpu.VMEM((nchunks, tq, DSA_SC), jnp.int32),
                        pltpu.VMEM((nchunks, tq, DSA_SC), F32),
                        pltpu.VMEM((tq, 1), jnp.int32),
                        pltpu.VMEM((tq, 1), jnp.int32),
                        pltpu.VMEM((n_heads * tq, HEAD_DIM), BF16),
                        pltpu.VMEM((n_heads * tq, LANES), F32),
                        pltpu.VMEM((n_heads * tq, 2 * HEAD_DIM), F32),
                        pltpu.VMEM((1, N_KV_HEADS * HEAD_DIM), F32)],
        compiler_params=_params("parallel", "arbitrary"),
        name="dsa_attention",
    )(qs, k_all, v_all, qi, ki_all, p3)


def _memattn_body(q_ref, g_ref, mk_ref, mv_ref, o_ref):
    for h in range(MEM_HEADS):
        sl = slice(h * HEAD_DIM, (h + 1) * HEAD_DIM)
        x = q_ref[0, :, sl]
        ms = jnp.mean(x * x, axis=-1, keepdims=True)
        qn = (x * lax.rsqrt(ms + EPS) * g_ref[...] * ATTN_SCALE).astype(BF16)
        lg = lax.dot_general(qn, mk_ref[0, :, sl], _NT, preferred_element_type=F32)
        m = jnp.max(lg, axis=-1, keepdims=True)
        p = jnp.exp(lg - m)
        l = jnp.sum(p, axis=-1, keepdims=True)
        o = lax.dot_general(p.astype(BF16), mv_ref[0, :, sl], _NN, preferred_element_type=F32)
        o_ref[0, :, sl] = (o / l).astype(o_ref.dtype)


def _mem_attention(qm, gain, mk, mv):
    b, t, w = qm.shape
    m = mk.shape[1]
    tq = _pick(t, (512, 256, 128, 64))
    return pl.pallas_call(
        _memattn_body,
        grid=(b, t // tq),
        in_specs=[pl.BlockSpec((1, tq, w), lambda i, j: (i, j, 0)),
                  pl.BlockSpec((1, HEAD_DIM), lambda i, j: (0, 0)),
                  pl.BlockSpec((1, m, w), lambda i, j: (i, 0, 0)),
                  pl.BlockSpec((1, m, w), lambda i, j: (i, 0, 0))],
        out_specs=pl.BlockSpec((1, tq, w), lambda i, j: (i, j, 0)),
        out_shape=jax.ShapeDtypeStruct((b, t, w), BF16),
        compiler_params=_params("parallel", "parallel"),
        name="mem_attention",
    )(qm, gain.reshape(1, HEAD_DIM), mk, mv)


PEER_HALF = PEER_QDIM // 2
_PEER_PAIRS = [(a, b) for a in range(PEER_TOPK) for b in range(PEER_TOPK) if (a + 1) * (b + 1) <= PEER_TOPK]


def _route_body(pq_ref, k1_ref, k2_ref, r2_ref, e2_ref, na_ref, e1_ref, rk1_ref, v1_ref, v2_ref, c_ref, *, tt):
    row = lax.broadcasted_iota(jnp.int32, (PEER_KEYS, tt), 0)
    for h in range(PEER_HEADS):
        for table, (kref, vref, lo) in enumerate(((k1_ref, v1_ref, 0), (k2_ref, v2_ref, PEER_HALF))):
            q = pq_ref[:, h * PEER_QDIM + lo:h * PEER_QDIM + lo + PEER_HALF]
            s = lax.dot_general(kref[h], q, _NT, preferred_element_type=F32)
            cur = s
            rank = jnp.full((PEER_KEYS, tt), float(PEER_TOPK), F32)
            for r in range(PEER_TOPK):
                m = jnp.max(cur, axis=0, keepdims=True)
                vref[r, h:h + 1, :] = m
                first = jnp.min(jnp.where(cur == m, row, PEER_KEYS), axis=0, keepdims=True)
                hit = row == first
                cur = jnp.where(hit, -jnp.inf, cur)
                rank = jnp.where(hit, float(r), rank)
            ex = jnp.exp(s - vref[0, h:h + 1, :])
            if table == 0:
                rk1_ref[h] = rank
                e1_ref[h] = ex
            else:
                r2_ref[h] = rank.astype(r2_ref.dtype)
                e2_ref[h] = ex.astype(e2_ref.dtype)
    for p, (a, b) in enumerate(_PEER_PAIRS):
        c_ref[p] = v1_ref[a] + v2_ref[b]
    call = c_ref[...]

    def tau_step(p, tau):
        cp = c_ref[p]
        cnt = jnp.sum((call >= cp[None]).astype(F32), axis=0)
        return jnp.maximum(tau, jnp.where(cnt >= PEER_TOPK, cp, -jnp.inf))

    tau = lax.fori_loop(0, len(_PEER_PAIRS), tau_step, jnp.full((PEER_HEADS, tt), -jnp.inf, F32))
    zero = jnp.zeros((PEER_HEADS, tt), F32)
    cnt_gt = [zero] * PEER_TOPK
    cnt_eq = [zero] * PEER_TOPK
    for p, (a, b) in enumerate(_PEER_PAIRS):
        cnt_gt[a] = cnt_gt[a] + (call[p] > tau).astype(F32)
        cnt_eq[a] = cnt_eq[a] + (call[p] == tau).astype(F32)
    rem = float(PEER_TOPK) - sum(cnt_gt)
    n_of = []
    for a in range(PEER_TOPK):
        take = jnp.minimum(cnt_eq[a], rem)
        rem = rem - take
        n_of.append(cnt_gt[a] + take)
    z = zero
    for p, (a, b) in enumerate(_PEER_PAIRS):
        z = z + jnp.where(n_of[a] > float(b), jnp.exp(call[p] - call[0]), 0.0)
    inv_z = 1.0 / z
    for h in range(PEER_HEADS):
        rk = rk1_ref[h]
        na = jnp.zeros((PEER_KEYS, tt), F32)
        for a in range(PEER_TOPK):
            na = jnp.where(rk == float(a), n_of[a][h:h + 1, :], na)
        na_ref[h] = na
        e1_ref[h] = e1_ref[h] * inv_z[h:h + 1, :]


def _peer_route(pq, k1, k2):
    n = pq.shape[0]
    tt = _pick(n, (256, 128))
    shape = (PEER_HEADS, PEER_KEYS, n)
    big_spec = pl.BlockSpec((PEER_HEADS, PEER_KEYS, tt), lambda i: (0, 0, i))
    kspec = pl.BlockSpec((PEER_HEADS, PEER_KEYS, PEER_HALF), lambda i: (0, 0, 0))
    return pl.pallas_call(
        functools.partial(_route_body, tt=tt),
        grid=(n // tt,),
        in_specs=[pl.BlockSpec((tt, PEER_HEADS * PEER_QDIM), lambda i: (i, 0)), kspec, kspec],
        out_specs=[big_spec, big_spec, big_spec, big_spec],
        out_shape=[jax.ShapeDtypeStruct(shape, BF16), jax.ShapeDtypeStruct(shape, BF16),
                   jax.ShapeDtypeStruct(shape, F32), jax.ShapeDtypeStruct(shape, F32)],
        scratch_shapes=[pltpu.VMEM((PEER_HEADS, PEER_KEYS, tt), F32),
                        pltpu.VMEM((PEER_TOPK, PEER_HEADS, tt), F32),
                        pltpu.VMEM((PEER_TOPK, PEER_HEADS, tt), F32),
                        pltpu.VMEM((len(_PEER_PAIRS), PEER_HEADS, tt), F32)],
        compiler_params=_params("parallel"),
        name="peer_route",
    )(pq, k1, k2)


PEER_EB = 1024


def _coef_body(u_ref, x_ref, r2_ref, e2_ref, na_ref, e1_ref, o_ref, *, tt):
    nb = PEER_EB // PEER_KEYS
    e = pl.program_id(1)
    act = lax.dot_general(u_ref[...], x_ref[...], _NT, preferred_element_type=F32)
    gdt = r2_ref.dtype
    for j in range(nb):
        i1 = e * nb + j
        gate = jnp.zeros((PEER_KEYS, tt), gdt)
        for h in range(PEER_HEADS):
            na = jnp.broadcast_to(na_ref[h, pl.ds(i1, 1), :].astype(gdt), (PEER_KEYS, tt))
            e1 = jnp.broadcast_to(e1_ref[h, pl.ds(i1, 1), :].astype(gdt), (PEER_KEYS, tt))
            gate = gate + jnp.where(r2_ref[h] < na, e1 * e2_ref[h], jnp.zeros((), gdt))
        a = act[j * PEER_KEYS:(j + 1) * PEER_KEYS, :]
        gelu = 0.5 * a * (1.0 + lax.erf(a * math.sqrt(0.5)))
        o_ref[j * PEER_KEYS:(j + 1) * PEER_KEYS, :] = (gate * gelu.astype(gdt)).astype(o_ref.dtype)


def _peer_coef(u_tab, xn, r2, e2, na, e1):
    ne, d = u_tab.shape
    n = xn.shape[0]
    tt = _pick(n, (512, 256, 128))
    big_spec = pl.BlockSpec((PEER_HEADS, PEER_KEYS, tt), lambda i, e: (0, 0, i))
    return pl.pallas_call(
        functools.partial(_coef_body, tt=tt),
        grid=(n // tt, ne // PEER_EB),
        in_specs=[pl.BlockSpec((PEER_EB, d), lambda i, e: (e, 0)),
                  pl.BlockSpec((tt, d), lambda i, e: (i, 0)),
                  big_spec, big_spec, big_spec, big_spec],
        out_specs=pl.BlockSpec((PEER_EB, tt), lambda i, e: (e, i)),
        out_shape=jax.ShapeDtypeStruct((ne, n), BF16),
        compiler_params=_params("parallel", "arbitrary"),
        name="peer_coef",
    )(u_tab, xn, r2, e2, na, e1)


def _round_up(x, m):
    return -(-x // m) * m


def _layer(x, pos0, conv_prev, k_past, v_past, ki_past, mem_k, mem_v, w):
    b, t, d = x.shape
    n = b * t
    ch = d // 2
    n_heads = (d - ch) // HEAD_DIM
    off = w["offsets"]
    pos = pos0 + jnp.arange(t, dtype=jnp.int32)
    rope = _rope_tables(pos)

    x2 = x.reshape(n, d)
    hn = _rmsnorm(x2, w["norm_mix_g"])
    tm = _pick(n, (1024, 512, 256, 128, 64))
    if tm > t:
        rope = tuple(jnp.tile(tab, (tm // t, 1)) for tab in rope)
    u = _proj_glu(hn, w["w_in"], off["a"], off["g"], ch, tm)
    qs = _proj_heads(hn, w["w_in"], off["q"], n_heads * HEAD_DIM, rope, tm, gain=w["q_norm_g"], scale=ATTN_SCALE)
    qi = _proj_heads(hn, w["w_in"], off["qi"], IDX_HEADS * IDX_DIM, rope, tm)
    k32, k16, v32, v16, ki32, ki16, wi = _proj_tail(hn, w["w_tail"], rope, tm, w["k_norm_g"])
    kvw = KV_W

    u3 = u.reshape(b, t, ch)
    prev = jnp.zeros((b, CONV_STATE, ch), F32) if conv_prev is None else conv_prev
    conv_out = _conv_module(u3, prev, w["dw_w"], w["dw_b"], w["conv_ln_g"], w["conv_ln_b"])
    conv_new = u3[:, t - CONV_STATE:, :]

    k3, v3, ki3 = k16.reshape(b, t, kvw), v16.reshape(b, t, kvw), ki16.reshape(b, t, IDX_DIM)
    s_valid = t
    if k_past is not None:
        s_valid = k_past.shape[1] + t
        k3, v3, ki3 = _append_cache(k_past, v_past, ki_past, k3, v3, ki3)
    s_pad = _round_up(k3.shape[1], DSA_SC)
    if s_pad != k3.shape[1]:
        padw = ((0, 0), (0, s_pad - k3.shape[1]), (0, 0))
        k3, v3, ki3 = jnp.pad(k3, padw), jnp.pad(v3, padw), jnp.pad(ki3, padw)
    topk = min(TOPK_MAX, s_valid // 4)
    attn = _dsa_attention(qs.reshape(b, t, n_heads * HEAD_DIM), k3, v3, qi.reshape(b, t, IDX_HEADS * IDX_DIM),
                          ki3, wi.reshape(b, t, LANES), 0, s_valid, topk, pos0)

    h1 = _matmul_cat(conv_out.reshape(n, ch), attn.reshape(n, n_heads * HEAD_DIM), w["w_out"], x2)

    hn2 = _rmsnorm(h1, w["norm_mem_g"])
    qm = _matmul(hn2, w["w_q_mem"])
    om = _mem_attention(qm.reshape(b, t, MEM_DIM), w["mem_q_norm_g"], mem_k, mem_v)
    h2 = _matmul(om.reshape(n, MEM_DIM), w["w_o_mem"], res=h1)

    hn3 = _rmsnorm(h2, w["norm_ffn_g"])
    pq = _matmul(hn3, w["peer_wq"], out_dtype=BF16)
    r2, e2, na, e1 = _peer_route(pq, w["peer_sub_k1"], w["peer_sub_k2"])
    coef_t = _peer_coef(w["peer_u"], hn3, r2, e2, na, e1)
    ne = coef_t.shape[0]
    y = _matmul(coef_t, w["peer_v"], res=h2, trans_a=True,
                tm=_pick(n, (1024, 512)), tn=_pick(d, (1024, 512)), tk=_pick(ne, (2048, 1024)))

    return (y.reshape(b, t, d), k32.reshape(b, t, N_KV_HEADS, HEAD_DIM), v32.reshape(b, t, N_KV_HEADS, HEAD_DIM),
            ki32.reshape(b, t, IDX_DIM), conv_new)


def _mem_kv(mem, mem_norm_g, w_kv, mem_k_norm_g):
    b, m, d = mem.shape
    mn = _rmsnorm(mem.reshape(b * m, d), mem_norm_g)
    mkv = _matmul(mn, w_kv)
    mk32, mk16 = _head_post(mkv, 0, MEM_HEADS, m, (F32, BF16), gain=mem_k_norm_g)
    mv32 = mkv[:, MEM_DIM:]
    return mk32.reshape(b, m, MEM_DIM), mv32.reshape(b, m, MEM_DIM), mk16.reshape(b, m, MEM_DIM)


def _prep_layer_weights(l, d, norm_mix_g, w_in, dw_w, dw_b, conv_ln_g, conv_ln_b, q_norm_g, k_norm_g, w_out,
                        norm_mem_g, w_q_mem, mem_q_norm_g, w_o_mem, norm_ffn_g, peer_wq, peer_sub_k1,
                        peer_sub_k2, peer_u, peer_v):
    ch = d // 2
    nq = d - ch
    niq = IDX_HEADS * IDX_DIM
    offsets = {"a": 0, "g": ch, "q": 2 * ch}
    offsets["k"] = offsets["q"] + nq
    offsets["qi"] = offsets["k"] + 2 * KV_W
    offsets["ki"] = offsets["qi"] + niq
    wl = w_in[l].astype(BF16)
    end = offsets["ki"] + IDX_DIM + IDX_HEADS
    w_tail = jnp.concatenate([wl[:, offsets["k"]:offsets["qi"]], wl[:, offsets["ki"]:end],
                              jnp.zeros((d, LANES - IDX_HEADS), BF16)], axis=1)
    return {
        "offsets": offsets,
        "w_in": wl,
        "w_tail": w_tail,
        "norm_mix_g": norm_mix_g[l], "dw_w": dw_w[l], "dw_b": dw_b[l],
        "conv_ln_g": conv_ln_g[l], "conv_ln_b": conv_ln_b[l],
        "q_norm_g": q_norm_g[l], "k_norm_g": k_norm_g[l],
        "w_out": w_out[l].astype(BF16),
        "norm_mem_g": norm_mem_g[l], "w_q_mem": w_q_mem[l].astype(BF16), "mem_q_norm_g": mem_q_norm_g[l],
        "w_o_mem": w_o_mem[l].astype(BF16),
        "norm_ffn_g": norm_ffn_g[l], "peer_wq": peer_wq[l].astype(BF16),
        "peer_sub_k1": peer_sub_k1[l].astype(BF16), "peer_sub_k2": peer_sub_k2[l].astype(BF16),
        "peer_u": peer_u[l].astype(BF16), "peer_v": peer_v[l].astype(BF16),
    }


def kernel(x_prompt, x_sample, mem_prompt, cache_k, cache_v, cache_k_idx, state_conv, cache_mem_k, cache_mem_v, norm_mix_g, w_in, dw_w, dw_b, conv_ln_g, conv_ln_b, q_norm_g, k_norm_g, w_out, norm_mem_g, mem_norm_g, w_q_mem, w_k_mem, w_v_mem, mem_q_norm_g, mem_k_norm_g, w_o_mem, norm_ffn_g, peer_wq, peer_sub_k1, peer_sub_k2, peer_u, peer_v):
    depth = w_in.shape[0]
    d = x_prompt.shape[-1]
    past_len = cache_k.shape[2]
    h_p, h_s = x_prompt, x_sample
    outs = [[] for _ in range(10)]
    for l in range(depth):
        w = _prep_layer_weights(l, d, norm_mix_g, w_in, dw_w, dw_b, conv_ln_g, conv_ln_b, q_norm_g, k_norm_g,
                                w_out, norm_mem_g, w_q_mem, mem_q_norm_g, w_o_mem, norm_ffn_g, peer_wq,
                                peer_sub_k1, peer_sub_k2, peer_u, peer_v)
        w_kv = jnp.concatenate([w_k_mem[l], w_v_mem[l]], axis=1).astype(BF16)
        mk32, mv32, mk16 = _mem_kv(mem_prompt, mem_norm_g[l], w_kv, mem_k_norm_g[l])
        h_p, kp, vp, kip, cp = _layer(h_p, 0, None, None, None, None, mk16, mv32.astype(BF16), w)
        bs, ms = cache_mem_k.shape[1], cache_mem_k.shape[2]
        h_s, ks, vs, kis, cs = _layer(h_s, past_len, state_conv[l], cache_k[l], cache_v[l], cache_k_idx[l],
                                      cache_mem_k[l].reshape(bs, ms, MEM_DIM).astype(BF16),
                                      cache_mem_v[l].reshape(bs, ms, MEM_DIM).astype(BF16), w)
        bp, mp = mem_prompt.shape[0], mem_prompt.shape[1]
        for lst, val in zip(outs, (kp, vp, kip, cp, mk32.reshape(bp, mp, MEM_HEADS, HEAD_DIM),
                                   mv32.reshape(bp, mp, MEM_HEADS, HEAD_DIM), ks, vs, kis, cs)):
            lst.append(val)
    return (h_p, h_s) + tuple(jnp.stack(lst) for lst in outs)
```
